```python
import jax, jax.numpy as jnp
from jax import lax
import numpy as np

D_MODEL = 1024
BATCH = 32
SEQ = 2048
DEPTH = 1

HEAD_DIM = 64
N_Q_HEADS = 16
N_KV_HEADS = 2
GQA_GROUP = N_Q_HEADS // N_KV_HEADS
WINDOW = 128
BLOCK = 128
ROPE_THETA = 10000.0
CONV_CH = D_MODEL
CONV_WIDTH = 31
D_FF = 2816
FFN_CONV_WIDTH = 3
RMS_EPS = 1e-6
LN_EPS = 1e-5

Q_W = N_Q_HEADS * HEAD_DIM
KV_W = N_KV_HEADS * HEAD_DIM
IN_SPLITS = (2 * CONV_CH, Q_W, KV_W, KV_W, D_MODEL, D_MODEL)
IN_WIDTH = sum(IN_SPLITS)

kernel_name = "hybrid_conv_swa_sink_convffn_block"


def rms_norm(x, g):
    xf = x.astype(jnp.float32)
    y = xf * lax.rsqrt(jnp.mean(xf * xf, axis=-1, keepdims=True) + RMS_EPS)
    return (y * g.astype(jnp.float32)).astype(x.dtype)


def layer_norm(x, g, b):
    xf = x.astype(jnp.float32)
    mu = jnp.mean(xf, axis=-1, keepdims=True)
    xc = xf - mu
    var = jnp.mean(xc * xc, axis=-1, keepdims=True)
    y = xc * lax.rsqrt(var + LN_EPS) * g.astype(jnp.float32) + b.astype(jnp.float32)
    return y.astype(x.dtype)


def causal_depthwise_conv(x, w, b):
    k = w.shape[0]
    c = x.shape[-1]
    y = lax.conv_general_dilated(
        x, w[:, None, :].astype(x.dtype), window_strides=(1,), padding=((k - 1, 0),),
        dimension_numbers=("NWC", "WIO", "NWC"), feature_group_count=c)
    return y + b.astype(x.dtype)


def rope_tables(positions):
    inv_freq = ROPE_THETA ** (-jnp.arange(0, HEAD_DIM, 2, dtype=jnp.float32) / HEAD_DIM)
    ang = positions.astype(jnp.float32)[..., None] * inv_freq
    return jnp.cos(ang)[:, :, None, :], jnp.sin(ang)[:, :, None, :]


def apply_rope(x, cos, sin):
    xf = x.astype(jnp.float32)
    x1, x2 = jnp.split(xf, 2, axis=-1)
    return jnp.concatenate([x1 * cos - x2 * sin, x2 * cos + x1 * sin], axis=-1).astype(x.dtype)


def sliding_window_sink_attention(q, k, v, sinks):
    b, s = q.shape[0], q.shape[1]
    nb = s // BLOCK
    qb = q.reshape(b, nb, BLOCK, N_KV_HEADS, GQA_GROUP, HEAD_DIM)

    def band(t):
        prev = jnp.pad(t, ((0, 0), (BLOCK, 0), (0, 0), (0, 0)))[:, :s]
        prev = prev.reshape(b, nb, BLOCK, N_KV_HEADS, HEAD_DIM)
        cur = t.reshape(b, nb, BLOCK, N_KV_HEADS, HEAD_DIM)
        return jnp.concatenate([prev, cur], axis=2)

    kb, vb = band(k), band(v)
    scores = jnp.einsum("bnqhgd,bnkhd->bnhgqk", qb, kb).astype(jnp.float32) * (HEAD_DIM ** -0.5)
    qi = jnp.arange(BLOCK)[:, None]
    kj = jnp.arange(2 * BLOCK)[None, :]
    rel = qi + BLOCK - kj
    in_window = (rel >= 0) & (rel < WINDOW)
    key_abs = jnp.arange(nb)[:, None, None] * BLOCK - BLOCK + kj[None]
    mask = in_window[None] & (key_abs >= 0)
    scores = jnp.where(mask[None, :, None, None], scores, -jnp.inf)
    sink = sinks.astype(jnp.float32).reshape(1, 1, N_KV_HEADS, GQA_GROUP, 1, 1)
    m = jnp.maximum(jnp.max(scores, axis=-1, keepdims=True), sink)
    p = jnp.exp(scores - m)
    probs = p / (jnp.sum(p, axis=-1, keepdims=True) + jnp.exp(sink - m))
    out = jnp.einsum("bnhgqk,bnkhd->bnqhgd", probs.astype(v.dtype), vb)
    return out.reshape(b, s, N_Q_HEADS * HEAD_DIM)


def setup_inputs(seed: int = 0) -> dict:
    key = jax.random.key(seed)
    ks = jax.random.split(key, 24)
    L, D = DEPTH, D_MODEL

    def nrm(k, shape, scale):
        return jax.random.normal(k, shape, dtype=jnp.float32) * scale

    def gain(k, n):
        return 1.0 + nrm(k, (L, n), 0.02)

    x = jax.random.normal(ks[0], (BATCH, SEQ, D), dtype=jnp.float32)
    start = jax.random.randint(ks[1], (BATCH, 1), 0, 1024, dtype=jnp.int32)
    positions = start + jnp.arange(SEQ, dtype=jnp.int32)[None, :]
    return {
        "x": x,
        "positions": positions,
        "ln_mix_pre": gain(ks[2], D),
        "w_in": nrm(ks[3], (L, D, IN_WIDTH), D ** -0.5),
        "b_gate": nrm(ks[4], (L, 2 * D), 0.02),
        "conv_dw_w": nrm(ks[5], (L, CONV_WIDTH, CONV_CH), CONV_WIDTH ** -0.5),
        "conv_dw_b": nrm(ks[6], (L, CONV_CH), 0.02),
        "conv_ln_g": gain(ks[7], CONV_CH),
        "conv_ln_b": nrm(ks[8], (L, CONV_CH), 0.02),
        "w_conv_out": nrm(ks[9], (L, CONV_CH, D), CONV_CH ** -0.5),
        "attn_sinks": nrm(ks[10], (L, N_Q_HEADS), 0.5),
        "w_attn_out": nrm(ks[11], (L, Q_W, D), Q_W ** -0.5),
        "w_out": nrm(ks[12], (L, D, D), D ** -0.5),
        "ln_mix_post": gain(ks[13], D),
        "ln_ffn_pre": gain(ks[14], D),
        "w_up": nrm(ks[15], (L, D, 2 * D_FF), D ** -0.5),
        "ffn_dw_w": nrm(ks[16], (L, FFN_CONV_WIDTH, 2 * D_FF), FFN_CONV_WIDTH ** -0.5),
        "ffn_dw_b": nrm(ks[17], (L, 2 * D_FF), 0.02),
        "w_down": nrm(ks[18], (L, D_FF, D), D_FF ** -0.5),
        "ln_ffn_post": gain(ks[19], D),
    }


def reference(x, positions, ln_mix_pre, w_in, b_gate, conv_dw_w, conv_dw_b, conv_ln_g, conv_ln_b,
              w_conv_out, attn_sinks, w_attn_out, w_out, ln_mix_post, ln_ffn_pre, w_up,
              ffn_dw_w, ffn_dw_b, w_down, ln_ffn_post):
    b, s, _ = x.shape
    cos, sin = rope_tables(positions)
    split_idx = list(np.cumsum(IN_SPLITS)[:-1])
    for l in range(DEPTH):
        h = rms_norm(x, ln_mix_pre[l])
        proj = h @ w_in[l]
        conv_in, q, k, v, g_conv_logit, g_attn_logit = jnp.split(proj, split_idx, axis=-1)
        gates = jax.nn.sigmoid(jnp.concatenate([g_conv_logit, g_attn_logit], axis=-1) + b_gate[l])
        g_conv, g_attn = jnp.split(gates, 2, axis=-1)

        a_val, a_gate = jnp.split(conv_in, 2, axis=-1)
        u = a_val * jax.nn.sigmoid(a_gate)
        u = causal_depthwise_conv(u, conv_dw_w[l], conv_dw_b[l])
        u = jax.nn.silu(layer_norm(u, conv_ln_g[l], conv_ln_b[l]))
        y_conv = u @ w_conv_out[l]

        q = apply_rope(q.reshape(b, s, N_Q_HEADS, HEAD_DIM), cos, sin)
        k = apply_rope(k.reshape(b, s, N_KV_HEADS, HEAD_DIM), cos, sin)
        v = v.reshape(b, s, N_KV_HEADS, HEAD_DIM)
        y_attn = sliding_window_sink_attention(q, k, v, attn_sinks[l]) @ w_attn_out[l]

        merged = g_conv * y_conv + g_attn * y_attn
        x = x + rms_norm(merged @ w_out[l], ln_mix_post[l])

        h = rms_norm(x, ln_ffn_pre[l])
        up = causal_depthwise_conv(h @ w_up[l], ffn_dw_w[l], ffn_dw_b[l])
        f_gate, f_val = jnp.split(up, 2, axis=-1)
        z = jax.nn.silu(f_gate) * f_val
        x = x + rms_norm(z @ w_down[l], ln_ffn_post[l])
    return x
```

```python
import functools

import jax
import jax.numpy as jnp
import numpy as np
from jax import lax
from jax.experimental import pallas as pl
from jax.experimental.pallas import tpu as pltpu

F32 = jnp.float32
BF16 = jnp.bfloat16

LANES = 128
SUBLANES = 8
VMEM_LIMIT_BYTES = 60000 * 1024

D_MODEL = 1024
HEAD_DIM = 64
N_Q_HEADS = 16
N_KV_HEADS = 2
ATT_BLOCK = 128
ROPE_THETA = 10000.0
CONV_WIDTH = 31
D_FF = 2816
FFN_CONV_WIDTH = 3
RMS_EPS = 1e-6
LN_EPS = 1e-5
Q_W = N_Q_HEADS * HEAD_DIM
KV_W = N_KV_HEADS * HEAD_DIM

SEQ_TILE = 256
CONV_HALO = 32
CONV_ROWS = 64
FFN_HALO = SUBLANES
FFN_CHUNK = 256
N_PAIRS = N_Q_HEADS // 2
PAIRS_PER_KV = N_PAIRS // N_KV_HEADS


def _rms_norm(x, g):
    return x * lax.rsqrt(jnp.mean(x * x, axis=-1, keepdims=True) + RMS_EPS) * g


def _dot(a, b):
    return jnp.dot(a, b, preferred_element_type=F32)


def _mix_kernel(sinks_ref, x_ref, pos_ref, invf_ref, g_pre_ref, w_in_ref, b_gate_ref, dww_ref, dwb_ref,
                lng_ref, lnb_ref, w_co_ref, w_ao_ref, w_o_ref, g_post_ref, o_ref,
                ubuf, ybuf, kbuf, vbuf, abuf):
    ts = SEQ_TILE
    sid = pl.program_id(1)

    @pl.when(sid == 0)
    def _():
        ubuf[0:CONV_HALO, :] = jnp.zeros((CONV_HALO, D_MODEL), F32)
        kbuf[:, 0:ATT_BLOCK, :] = jnp.zeros((4, ATT_BLOCK, LANES), BF16)
        vbuf[:, 0:ATT_BLOCK, :] = jnp.zeros((4, ATT_BLOCK, LANES), BF16)

    x = x_ref[0]
    h = _rms_norm(x, g_pre_ref[...]).astype(BF16)

    a = _dot(h, w_in_ref[:, 0:2 * D_MODEL])
    ubuf[CONV_HALO:CONV_HALO + ts, :] = a[:, :D_MODEL] * jax.nn.sigmoid(a[:, D_MODEL:])

    win = CONV_ROWS + CONV_HALO

    def conv_rows(ci, carry):
        base = pl.multiple_of(ci * CONV_ROWS, CONV_ROWS)
        for lc in range(D_MODEL // LANES):
            lanes = slice(lc * LANES, (lc + 1) * LANES)
            window = ubuf[pl.ds(base, win), lanes]
            acc = jnp.broadcast_to(dwb_ref[:, lanes], (CONV_ROWS, LANES))
            for r in range(SUBLANES):
                shifted = window if r == 0 else pltpu.roll(window, win - r, axis=0)
                for q in range(CONV_HALO // SUBLANES + 1):
                    k = SUBLANES * q + r - (CONV_HALO - CONV_WIDTH + 1)
                    if 0 <= k < CONV_WIDTH:
                        acc = acc + dww_ref[k:k + 1, lanes] * shifted[SUBLANES * q:SUBLANES * q + CONV_ROWS]
            ybuf[pl.ds(base, CONV_ROWS), lanes] = acc
        return carry

    lax.fori_loop(0, ts // CONV_ROWS, conv_rows, 0)
    ubuf[0:CONV_HALO, :] = ubuf[ts:ts + CONV_HALO, :]

    y = ybuf[...]
    mu = jnp.mean(y, axis=-1, keepdims=True)
    yc = y - mu
    var = jnp.mean(yc * yc, axis=-1, keepdims=True)
    yn = yc * lax.rsqrt(var + LN_EPS) * lng_ref[...] + lnb_ref[...]
    y_conv = _dot((yn * jax.nn.sigmoid(yn)).astype(BF16), w_co_ref[...])

    qkv = _dot(h, w_in_ref[:, 2 * D_MODEL:2 * D_MODEL + Q_W + 2 * KV_W])
    ang = pos_ref[0].astype(F32) * invf_ref[...]
    lane = lax.broadcasted_iota(jnp.int32, (ts, LANES), 1)
    first_half = (lane & (HEAD_DIM // 2)) == 0
    cos = jnp.cos(ang)
    sin = jnp.sin(ang)
    sin_signed = jnp.where(first_half, -sin, sin)

    def rope(t):
        swapped = jnp.where(first_half, pltpu.roll(t, LANES - HEAD_DIM // 2, axis=1),
                            pltpu.roll(t, HEAD_DIM // 2, axis=1))
        return t * cos + swapped * sin_signed

    q_cols = [(rope(qkv[:, p * LANES:(p + 1) * LANES]) * (HEAD_DIM ** -0.5)).astype(BF16)
              for p in range(N_PAIRS)]
    k_rot = rope(qkv[:, Q_W:Q_W + KV_W])
    v_raw = qkv[:, Q_W + KV_W:Q_W + 2 * KV_W]

    low = lane < HEAD_DIM
    zero = jnp.zeros((ts, LANES), F32)
    cur = slice(ATT_BLOCK, ATT_BLOCK + ts)
    for buf, t in ((kbuf, k_rot), (vbuf, v_raw)):
        t_sw = pltpu.roll(t, HEAD_DIM, axis=1)
        buf[0, cur, :] = jnp.where(low, t, zero).astype(BF16)
        buf[1, cur, :] = jnp.where(low, zero, t_sw).astype(BF16)
        buf[2, cur, :] = jnp.where(low, t_sw, zero).astype(BF16)
        buf[3, cur, :] = jnp.where(low, zero, t).astype(BF16)

    band = 2 * ATT_BLOCK
    qi = lax.broadcasted_iota(jnp.int32, (ATT_BLOCK, band), 0)
    kj = lax.broadcasted_iota(jnp.int32, (ATT_BLOCK, band), 1)
    rel = qi + ATT_BLOCK - kj
    band_mask = (rel >= 0) & (rel < ATT_BLOCK)
    first_mask = band_mask & (kj >= jnp.where(sid > 0, 0, ATT_BLOCK))
    sink_col = kj == 0
    vrow = lax.broadcasted_iota(jnp.int32, (2 * band, LANES), 0)
    v_keep = (vrow != 0) & (vrow != band)

    for nb in range(ts // ATT_BLOCK):
        rows = slice(nb * ATT_BLOCK, (nb + 1) * ATT_BLOCK)
        brows = slice(nb * ATT_BLOCK, nb * ATT_BLOCK + band)
        mask = first_mask if nb == 0 else band_mask
        for g in range(N_KV_HEADS):
            kt = jnp.concatenate([kbuf[2 * g, brows, :], kbuf[2 * g + 1, brows, :]], axis=0)
            vt = jnp.concatenate([vbuf[2 * g, brows, :], vbuf[2 * g + 1, brows, :]], axis=0)
            vt = jnp.where(v_keep, vt, jnp.zeros_like(vt))
            q4 = jnp.concatenate([q_cols[PAIRS_PER_KV * g + i][rows] for i in range(PAIRS_PER_KV)], axis=0)
            s = lax.dot_general(q4, kt, (((1,), (1,)), ((), ())), preferred_element_type=F32)
            probs = []
            for i in range(PAIRS_PER_KV):
                halves = []
                for hf in range(2):
                    sink = sinks_ref[2 * (PAIRS_PER_KV * g + i) + hf]
                    sh = s[i * ATT_BLOCK:(i + 1) * ATT_BLOCK, hf * band:(hf + 1) * band]
                    sh = jnp.where(mask, sh, jnp.where(sink_col, sink, -jnp.inf))
                    m = jnp.max(sh, axis=-1, keepdims=True)
                    p = jnp.exp(sh - m)
                    halves.append((p / jnp.sum(p, axis=-1, keepdims=True)).astype(BF16))
                probs.append(jnp.concatenate(halves, axis=1))
            o4 = _dot(jnp.concatenate(probs, axis=0), vt)
            for i in range(PAIRS_PER_KV):
                p_idx = PAIRS_PER_KV * g + i
                abuf[rows, p_idx * LANES:(p_idx + 1) * LANES] = o4[i * ATT_BLOCK:(i + 1) * ATT_BLOCK].astype(BF16)

    carry_rows = slice(ts, ts + ATT_BLOCK)
    for buf in (kbuf, vbuf):
        for slot in range(4):
            buf[slot, 0:ATT_BLOCK, :] = buf[slot, carry_rows, :]

    y_attn = _dot(abuf[...], w_ao_ref[...])

    gates = jax.nn.sigmoid(_dot(h, w_in_ref[:, 2 * D_MODEL + Q_W + 2 * KV_W:]) + b_gate_ref[...])
    merged = gates[:, :D_MODEL] * y_conv + gates[:, D_MODEL:] * y_attn
    o_ref[0] = x + _rms_norm(_dot(merged.astype(BF16), w_o_ref[...]), g_post_ref[...])


def _ffn_kernel(x_ref, g_pre_ref, w_up_ref, dww_ref, dwb_ref, w_dn_ref, g_post_ref, o_ref, pbuf):
    ts = SEQ_TILE
    sid = pl.program_id(1)

    @pl.when(sid == 0)
    def _():
        pbuf[:, 0:FFN_HALO, :] = jnp.zeros((D_FF // FFN_CHUNK, FFN_HALO, 2 * FFN_CHUNK), F32)

    x = x_ref[0]
    h = _rms_norm(x, g_pre_ref[...]).astype(BF16)
    n = ts + FFN_HALO
    acc = jnp.zeros((ts, D_MODEL), F32)
    for c in range(D_FF // FFN_CHUNK):
        cols = slice(c * 2 * FFN_CHUNK, (c + 1) * 2 * FFN_CHUNK)
        pbuf[c, FFN_HALO:n, :] = _dot(h, w_up_ref[:, cols])
        p = pbuf[c]
        up = dwb_ref[:, cols] + dww_ref[2:3, cols] * p[FFN_HALO:]
        up = up + dww_ref[1:2, cols] * pltpu.roll(p, n - (FFN_HALO - 1), axis=0)[:ts]
        up = up + dww_ref[0:1, cols] * pltpu.roll(p, n - (FFN_HALO - 2), axis=0)[:ts]
        pbuf[c, 0:FFN_HALO, :] = pbuf[c, ts:n, :]
        gate = up[:, :FFN_CHUNK]
        z = gate * jax.nn.sigmoid(gate) * up[:, FFN_CHUNK:]
        acc = acc + _dot(z.astype(BF16), w_dn_ref[c * FFN_CHUNK:(c + 1) * FFN_CHUNK, :])
    o_ref[0] = x + _rms_norm(acc, g_post_ref[...])


def _const_spec(shape):
    return pl.BlockSpec(shape, lambda b, s: (0,) * len(shape), pipeline_mode=pl.Buffered(1))


def _tile_spec(width):
    return pl.BlockSpec((1, SEQ_TILE, width), lambda b, s: (b, s, 0))


def _compiler_params():
    return pltpu.CompilerParams(dimension_semantics=("arbitrary", "arbitrary"),
                                vmem_limit_bytes=VMEM_LIMIT_BYTES)


def _interleave_halves(w, chunk):
    half = w.shape[-1] // 2
    lead = w.shape[:-1]
    g = w[..., :half].reshape(*lead, half // chunk, chunk)
    v = w[..., half:].reshape(*lead, half // chunk, chunk)
    return jnp.concatenate([g, v], axis=-1).reshape(*lead, 2 * half)


def kernel(x, positions, ln_mix_pre, w_in, b_gate, conv_dw_w, conv_dw_b, conv_ln_g, conv_ln_b, w_conv_out, attn_sinks, w_attn_out, w_out, ln_mix_post, ln_ffn_pre, w_up, ffn_dw_w, ffn_dw_b, w_down, ln_ffn_post):
    bsz, seq, d = x.shape
    depth = w_in.shape[0]
    assert d == D_MODEL and seq % SEQ_TILE == 0 and SEQ_TILE % ATT_BLOCK == 0
    grid = (bsz, seq // SEQ_TILE)
    ts = SEQ_TILE

    inv_freq = ROPE_THETA ** (-jnp.arange(0, HEAD_DIM, 2, dtype=F32) / HEAD_DIM)
    inv_freq = jnp.tile(inv_freq, LANES // (HEAD_DIM // 2))[None, :]
    pos3 = positions.reshape(bsz, seq, 1)

    row = lambda v: v.reshape(1, -1)
    for l in range(depth):
        x = pl.pallas_call(
            _mix_kernel,
            grid=grid,
            in_specs=[
                pl.BlockSpec(memory_space=pltpu.SMEM),
                _tile_spec(D_MODEL),
                _tile_spec(1),
                _const_spec((1, LANES)),
                _const_spec((1, D_MODEL)),
                _const_spec(w_in.shape[1:]),
                _const_spec((1, 2 * D_MODEL)),
                _const_spec((CONV_WIDTH, D_MODEL)),
                _const_spec((1, D_MODEL)),
                _const_spec((1, D_MODEL)),
                _const_spec((1, D_MODEL)),
                _const_spec((D_MODEL, D_MODEL)),
                _const_spec((Q_W, D_MODEL)),
                _const_spec((D_MODEL, D_MODEL)),
                _const_spec((1, D_MODEL)),
            ],
            out_specs=_tile_spec(D_MODEL),
            out_shape=jax.ShapeDtypeStruct(x.shape, F32),
            scratch_shapes=[
                pltpu.VMEM((CONV_HALO + ts, D_MODEL), F32),
                pltpu.VMEM((ts, D_MODEL), F32),
                pltpu.VMEM((4, ATT_BLOCK + ts, LANES), BF16),
                pltpu.VMEM((4, ATT_BLOCK + ts, LANES), BF16),
                pltpu.VMEM((ts, Q_W), BF16),
            ],
            compiler_params=_compiler_params(),
            name="mix",
        )(attn_sinks[l], x, pos3, inv_freq, row(ln_mix_pre[l]), w_in[l].astype(BF16), row(b_gate[l]),
          conv_dw_w[l], row(conv_dw_b[l]), row(conv_ln_g[l]), row(conv_ln_b[l]),
          w_conv_out[l].astype(BF16), w_attn_out[l].astype(BF16), w_out[l].astype(BF16), row(ln_mix_post[l]))

        x = pl.pallas_call(
            _ffn_kernel,
            grid=grid,
            in_specs=[
                _tile_spec(D_MODEL),
                _const_spec((1, D_MODEL)),
                _const_spec((D_MODEL, 2 * D_FF)),
                _const_spec((FFN_CONV_WIDTH, 2 * D_FF)),
                _const_spec((1, 2 * D_FF)),
                _const_spec((D_FF, D_MODEL)),
                _const_spec((1, D_MODEL)),
            ],
            out_specs=_tile_spec(D_MODEL),
            out_shape=jax.ShapeDtypeStruct(x.shape, F32),
            scratch_shapes=[pltpu.VMEM((D_FF // FFN_CHUNK, FFN_HALO + ts, 2 * FFN_CHUNK), F32)],
            compiler_params=_compiler_params(),
            name="ffn",
        )(x, row(ln_ffn_pre[l]), _interleave_halves(w_up[l], FFN_CHUNK).astype(BF16),
          _interleave_halves(ffn_dw_w[l], FFN_CHUNK), row(_interleave_halves(ffn_dw_b[l], FFN_CHUNK)),
          w_down[l].astype(BF16), row(ln_ffn_post[l]))
    return x
```

```python
import functools

import jax
import jax.numpy as jnp
import numpy as np
from jax import lax
from jax.experimental import pallas as pl
from jax.experimental.pallas import tpu as pltpu

F32 = jnp.float32
BF16 = jnp.bfloat16

LANES = 128
SUBLANES = 8
VMEM_LIMIT_BYTES = 60000 * 1024

D_MODEL = 1024
HEAD_DIM = 64
N_Q_HEADS = 16
N_KV_HEADS = 2
ATT_BLOCK = 128
ROPE_THETA = 10000.0
CONV_WIDTH = 31
D_FF = 2816
FFN_CONV_WIDTH = 3
RMS_EPS = 1e-6
LN_EPS = 1e-5
Q_W = N_Q_HEADS * HEAD_DIM
KV_W = N_KV_HEADS * HEAD_DIM

SEQ_TILE = 256
CONV_HALO = 32
CONV_ROWS = 64
FFN_HALO = SUBLANES
FFN_CHUNK = 256
FFN_LOOKAHEAD = 2
N_PAIRS = N_Q_HEADS // 2
PAIRS_PER_KV = N_PAIRS // N_KV_HEADS


def _rms_norm(x, g):
    return x * lax.rsqrt(jnp.mean(x * x, axis=-1, keepdims=True) + RMS_EPS) * g


def _dot(a, b):
    return jnp.dot(a, b, preferred_element_type=F32)


def _mix_kernel(sinks_ref, x_ref, pos_ref, invf_ref, g_pre_ref, w_in_ref, b_gate_ref, dww_ref, dwb_ref,
                lng_ref, lnb_ref, w_co_ref, w_ao_ref, w_o_ref, g_post_ref, o_ref,
                ubuf, ybuf, kbuf, vbuf, abuf):
    ts = SEQ_TILE
    sid = pl.program_id(1)

    @pl.when(sid == 0)
    def _():
        ubuf[0:CONV_HALO, :] = jnp.zeros((CONV_HALO, D_MODEL), F32)
        kbuf[:, 0:ATT_BLOCK, :] = jnp.zeros((4, ATT_BLOCK, LANES), BF16)
        vbuf[:, 0:ATT_BLOCK, :] = jnp.zeros((4, ATT_BLOCK, LANES), BF16)

    x = x_ref[0]
    h = _rms_norm(x, g_pre_ref[...]).astype(BF16)

    a = _dot(h, w_in_ref[:, 0:2 * D_MODEL])
    qkv = _dot(h, w_in_ref[:, 2 * D_MODEL:2 * D_MODEL + Q_W + 2 * KV_W])
    gate_logits = _dot(h, w_in_ref[:, 2 * D_MODEL + Q_W + 2 * KV_W:])

    ubuf[CONV_HALO:CONV_HALO + ts, :] = a[:, :D_MODEL] * jax.nn.sigmoid(a[:, D_MODEL:])

    win = CONV_ROWS + CONV_HALO
    for base in range(0, ts, CONV_ROWS):
        for lc in range(D_MODEL // LANES):
            lanes = slice(lc * LANES, (lc + 1) * LANES)
            window = ubuf[base:base + win, lanes]
            acc = jnp.broadcast_to(dwb_ref[:, lanes], (CONV_ROWS, LANES))
            for r in range(SUBLANES):
                shifted = window if r == 0 else pltpu.roll(window, win - r, axis=0)
                for q in range(CONV_HALO // SUBLANES + 1):
                    k = SUBLANES * q + r - (CONV_HALO - CONV_WIDTH + 1)
                    if 0 <= k < CONV_WIDTH:
                        acc = acc + dww_ref[k:k + 1, lanes] * shifted[SUBLANES * q:SUBLANES * q + CONV_ROWS]
            ybuf[base:base + CONV_ROWS, lanes] = acc
    ubuf[0:CONV_HALO, :] = ubuf[ts:ts + CONV_HALO, :]

    y = ybuf[...]
    mu = jnp.mean(y, axis=-1, keepdims=True)
    yc = y - mu
    var = jnp.mean(yc * yc, axis=-1, keepdims=True)
    yn = yc * lax.rsqrt(var + LN_EPS) * lng_ref[...] + lnb_ref[...]
    conv_act = (yn * jax.nn.sigmoid(yn)).astype(BF16)

    ang = pos_ref[0].astype(F32) * invf_ref[...]
    lane = lax.broadcasted_iota(jnp.int32, (ts, LANES), 1)
    first_half = (lane & (HEAD_DIM // 2)) == 0
    cos = jnp.cos(ang)
    sin = jnp.sin(ang)
    sin_signed = jnp.where(first_half, -sin, sin)

    def rope(t):
        swapped = jnp.where(first_half, pltpu.roll(t, LANES - HEAD_DIM // 2, axis=1),
                            pltpu.roll(t, HEAD_DIM // 2, axis=1))
        return t * cos + swapped * sin_signed

    q_cols = [(rope(qkv[:, p * LANES:(p + 1) * LANES]) * (HEAD_DIM ** -0.5)).astype(BF16)
              for p in range(N_PAIRS)]
    k_rot = rope(qkv[:, Q_W:Q_W + KV_W])
    v_raw = qkv[:, Q_W + KV_W:Q_W + 2 * KV_W]

    low = lane < HEAD_DIM
    zero = jnp.zeros((ts, LANES), F32)
    cur = slice(ATT_BLOCK, ATT_BLOCK + ts)
    for buf, t in ((kbuf, k_rot), (vbuf, v_raw)):
        t_sw = pltpu.roll(t, HEAD_DIM, axis=1)
        buf[0, cur, :] = jnp.where(low, t, zero).astype(BF16)
        buf[1, cur, :] = jnp.where(low, zero, t_sw).astype(BF16)
        buf[2, cur, :] = jnp.where(low, t_sw, zero).astype(BF16)
        buf[3, cur, :] = jnp.where(low, zero, t).astype(BF16)

    band = 2 * ATT_BLOCK
    qi = lax.broadcasted_iota(jnp.int32, (ATT_BLOCK, band), 0)
    kj = lax.broadcasted_iota(jnp.int32, (ATT_BLOCK, band), 1)
    rel = qi + ATT_BLOCK - kj
    band_mask = (rel >= 0) & (rel < ATT_BLOCK)
    first_mask = band_mask & (kj >= jnp.where(sid > 0, 0, ATT_BLOCK))
    sink_col = kj == 0
    vrow = lax.broadcasted_iota(jnp.int32, (2 * band, LANES), 0)
    v_keep = (vrow != 0) & (vrow != band)

    for nb in range(ts // ATT_BLOCK):
        rows = slice(nb * ATT_BLOCK, (nb + 1) * ATT_BLOCK)
        brows = slice(nb * ATT_BLOCK, nb * ATT_BLOCK + band)
        mask = first_mask if nb == 0 else band_mask
        for g in range(N_KV_HEADS):
            kt = jnp.concatenate([kbuf[2 * g, brows, :], kbuf[2 * g + 1, brows, :]], axis=0)
            vt = jnp.concatenate([vbuf[2 * g, brows, :], vbuf[2 * g + 1, brows, :]], axis=0)
            vt = jnp.where(v_keep, vt, jnp.zeros_like(vt))
            q4 = jnp.concatenate([q_cols[PAIRS_PER_KV * g + i][rows] for i in range(PAIRS_PER_KV)], axis=0)
            s = lax.dot_general(q4, kt, (((1,), (1,)), ((), ())), preferred_element_type=F32)
            probs = []
            for i in range(PAIRS_PER_KV):
                halves = []
                for hf in range(2):
                    sink = sinks_ref[2 * (PAIRS_PER_KV * g + i) + hf]
                    sh = s[i * ATT_BLOCK:(i + 1) * ATT_BLOCK, hf * band:(hf + 1) * band]
                    sh = jnp.where(mask, sh, jnp.where(sink_col, sink, -jnp.inf))
                    m = jnp.max(sh, axis=-1, keepdims=True)
                    p = jnp.exp(sh - m)
                    halves.append((p / jnp.sum(p, axis=-1, keepdims=True)).astype(BF16))
                probs.append(jnp.concatenate(halves, axis=1))
            o4 = _dot(jnp.concatenate(probs, axis=0), vt)
            for i in range(PAIRS_PER_KV):
                p_idx = PAIRS_PER_KV * g + i
                abuf[rows, p_idx * LANES:(p_idx + 1) * LANES] = o4[i * ATT_BLOCK:(i + 1) * ATT_BLOCK].astype(BF16)

    carry_rows = slice(ts, ts + ATT_BLOCK)
    for buf in (kbuf, vbuf):
        for slot in range(4):
            buf[slot, 0:ATT_BLOCK, :] = buf[slot, carry_rows, :]

    y_attn = _dot(abuf[...], w_ao_ref[...])
    y_conv = _dot(conv_act, w_co_ref[...])

    gates = jax.nn.sigmoid(gate_logits + b_gate_ref[...])
    merged = gates[:, :D_MODEL] * y_conv + gates[:, D_MODEL:] * y_attn
    o_ref[0] = x + _rms_norm(_dot(merged.astype(BF16), w_o_ref[...]), g_post_ref[...])


def _ffn_kernel(x_ref, g_pre_ref, w_up_ref, dww_ref, dwb_ref, w_dn_ref, g_post_ref, o_ref, pbuf):
    ts = SEQ_TILE
    sid = pl.program_id(1)

    @pl.when(sid == 0)
    def _():
        pbuf[:, 0:FFN_HALO, :] = jnp.zeros((D_FF // FFN_CHUNK, FFN_HALO, 2 * FFN_CHUNK), F32)

    x = x_ref[0]
    h = _rms_norm(x, g_pre_ref[...]).astype(BF16)
    n = ts + FFN_HALO
    n_chunks = D_FF // FFN_CHUNK
    chunk_cols = lambda c: slice(c * 2 * FFN_CHUNK, (c + 1) * 2 * FFN_CHUNK)
    acc = jnp.zeros((ts, D_MODEL), F32)
    for c in range(min(FFN_LOOKAHEAD, n_chunks)):
        pbuf[c, FFN_HALO:n, :] = _dot(h, w_up_ref[:, chunk_cols(c)])
    for c in range(n_chunks):
        cols = chunk_cols(c)
        if c + FFN_LOOKAHEAD < n_chunks:
            pbuf[c + FFN_LOOKAHEAD, FFN_HALO:n, :] = _dot(h, w_up_ref[:, chunk_cols(c + FFN_LOOKAHEAD)])
        p = pbuf[c]
        up = dwb_ref[:, cols] + dww_ref[2:3, cols] * p[FFN_HALO:]
        up = up + dww_ref[1:2, cols] * pltpu.roll(p, n - (FFN_HALO - 1), axis=0)[:ts]
        up = up + dww_ref[0:1, cols] * pltpu.roll(p, n - (FFN_HALO - 2), axis=0)[:ts]
        pbuf[c, 0:FFN_HALO, :] = pbuf[c, ts:n, :]
        gate = up[:, :FFN_CHUNK]
        z = gate * jax.nn.sigmoid(gate) * up[:, FFN_CHUNK:]
        acc = acc + _dot(z.astype(BF16), w_dn_ref[c * FFN_CHUNK:(c + 1) * FFN_CHUNK, :])
    o_ref[0] = x + _rms_norm(acc, g_post_ref[...])


def _const_spec(shape):
    return pl.BlockSpec(shape, lambda b, s: (0,) * len(shape), pipeline_mode=pl.Buffered(1))


def _tile_spec(width):
    return pl.BlockSpec((1, SEQ_TILE, width), lambda b, s: (b, s, 0))


def _compiler_params():
    return pltpu.CompilerParams(dimension_semantics=("arbitrary", "arbitrary"),
                                vmem_limit_bytes=VMEM_LIMIT_BYTES)


def _interleave_halves(w, chunk):
    half = w.shape[-1] // 2
    lead = w.shape[:-1]
    g = w[..., :half].reshape(*lead, half // chunk, chunk)
    v = w[..., half:].reshape(*lead, half // chunk, chunk)
    return jnp.concatenate([g, v], axis=-1).reshape(*lead, 2 * half)


def kernel(x, positions, ln_mix_pre, w_in, b_gate, conv_dw_w, conv_dw_b, conv_ln_g, conv_ln_b, w_conv_out, attn_sinks, w_attn_out, w_out, ln_mix_post, ln_ffn_pre, w_up, ffn_dw_w, ffn_dw_b, w_down, ln_ffn_post):
    bsz, seq, d = x.shape
    depth = w_in.shape[0]
    assert d == D_MODEL and seq % SEQ_TILE == 0 and SEQ_TILE % ATT_BLOCK == 0
    grid = (bsz, seq // SEQ_TILE)
    ts = SEQ_TILE

    inv_freq = ROPE_THETA ** (-jnp.arange(0, HEAD_DIM, 2, dtype=F32) / HEAD_DIM)
    inv_freq = jnp.tile(inv_freq, LANES // (HEAD_DIM // 2))[None, :]
    pos3 = positions.reshape(bsz, seq, 1)

    row = lambda v: v.reshape(1, -1)
    for l in range(depth):
        x = pl.pallas_call(
            _mix_kernel,
            grid=grid,
            in_specs=[
                pl.BlockSpec(memory_space=pltpu.SMEM),
                _tile_spec(D_MODEL),
                _tile_spec(1),
                _const_spec((1, LANES)),
                _const_spec((1, D_MODEL)),
                _const_spec(w_in.shape[1:]),
                _const_spec((1, 2 * D_MODEL)),
                _const_spec((CONV_WIDTH, D_MODEL)),
                _const_spec((1, D_MODEL)),
                _const_spec((1, D_MODEL)),
                _const_spec((1, D_MODEL)),
                _const_spec((D_MODEL, D_MODEL)),
                _const_spec((Q_W, D_MODEL)),
                _const_spec((D_MODEL, D_MODEL)),
                _const_spec((1, D_MODEL)),
            ],
            out_specs=_tile_spec(D_MODEL),
            out_shape=jax.ShapeDtypeStruct(x.shape, F32),
            scratch_shapes=[
                pltpu.VMEM((CONV_HALO + ts, D_MODEL), F32),
                pltpu.VMEM((ts, D_MODEL), F32),
                pltpu.VMEM((4, ATT_BLOCK + ts, LANES), BF16),
                pltpu.VMEM((4, ATT_BLOCK + ts, LANES), BF16),
                pltpu.VMEM((ts, Q_W), BF16),
            ],
            compiler_params=_compiler_params(),
            name="mix",
        )(attn_sinks[l], x, pos3, inv_freq, row(ln_mix_pre[l]), w_in[l].astype(BF16), row(b_gate[l]),
          conv_dw_w[l], row(conv_dw_b[l]), row(conv_ln_g[l]), row(conv_ln_b[l]),
          w_conv_out[l].astype(BF16), w_attn_out[l].astype(BF16), w_out[l].astype(BF16), row(ln_mix_post[l]))

        x = pl.pallas_call(
            _ffn_kernel,
            grid=grid,
            in_specs=[
                _tile_spec(D_MODEL),
                _const_spec((1, D_MODEL)),
                _const_spec((D_MODEL, 2 * D_FF)),
                _const_spec((FFN_CONV_WIDTH, 2 * D_FF)),
                _const_spec((1, 2 * D_FF)),
                _const_spec((D_FF, D_MODEL)),
                _const_spec((1, D_MODEL)),
            ],
            out_specs=_tile_spec(D_MODEL),
            out_shape=jax.ShapeDtypeStruct(x.shape, F32),
            scratch_shapes=[pltpu.VMEM((D_FF // FFN_CHUNK, FFN_HALO + ts, 2 * FFN_CHUNK), F32)],
            compiler_params=_compiler_params(),
            name="ffn",
        )(x, row(ln_ffn_pre[l]), _interleave_halves(w_up[l], FFN_CHUNK).astype(BF16),
          _interleave_halves(ffn_dw_w[l], FFN_CHUNK), row(_interleave_halves(ffn_dw_b[l], FFN_CHUNK)),
          w_down[l].astype(BF16), row(ln_ffn_post[l]))
    return x
```

```python
import collections
import functools

import jax
import jax.numpy as jnp
from jax import lax
from jax.experimental import pallas as pl
from jax.experimental.pallas import tpu as pltpu

F32 = jnp.float32
BF16 = jnp.bfloat16

LANES = 128
SUBLANES = 8
VMEM_LIMIT_BYTES = 60000 * 1024

D_MODEL = 1024
HEAD_DIM = 64
N_Q_HEADS = 16
N_KV_HEADS = 2
ATT_BLOCK = 128
ROPE_THETA = 10000.0
CONV_WIDTH = 31
D_FF = 2816
FFN_CONV_WIDTH = 3
RMS_EPS = 1e-6
LN_EPS = 1e-5
Q_W = N_Q_HEADS * HEAD_DIM
KV_W = N_KV_HEADS * HEAD_DIM

MIX_TILE = 256
CONV_HALO = 32
CONV_TOKENS = 8
FFN_TILE = 256
FFN_HALO = SUBLANES
FFN_CHUNK = 256
FFN_LOOKAHEAD = 2
N_PAIRS = N_Q_HEADS // 2
PAIRS_PER_KV = N_PAIRS // N_KV_HEADS
BAND = 2 * ATT_BLOCK

MixState = collections.namedtuple("MixState", "u q k v g x y ab")
MixRaw = collections.namedtuple("MixRaw", "s a qkv gl ya yc o")


def _rms_norm(x, g):
    return x * lax.rsqrt(jnp.mean(x * x, axis=-1, keepdims=True) + RMS_EPS) * g


def _dot(a, b):
    return jnp.dot(a, b, preferred_element_type=F32)


def _mix_pair(x_ref, pos_ref, out_ref, wr, rd, raw, first_p, first_f, c):
    ts = MIX_TILE
    x = x_ref[0]
    wr.x[...] = x
    h = _rms_norm(x, c.g_pre[...]).astype(BF16)

    qi = lax.broadcasted_iota(jnp.int32, (ATT_BLOCK, BAND), 0)
    kj = lax.broadcasted_iota(jnp.int32, (ATT_BLOCK, BAND), 1)
    rel = qi + ATT_BLOCK - kj
    band_mask = (rel >= 0) & (rel < ATT_BLOCK)
    first_mask = band_mask & (kj >= jnp.where(first_f, ATT_BLOCK, 0))
    sink_col = kj == 0
    vrow = lax.broadcasted_iota(jnp.int32, (2 * BAND, LANES), 0)
    v_keep = (vrow != 0) & (vrow != BAND)

    def scores(nb, g):
        rows = slice(nb * ATT_BLOCK, (nb + 1) * ATT_BLOCK)
        brows = slice(nb * ATT_BLOCK, nb * ATT_BLOCK + BAND)
        kt = jnp.concatenate([rd.k[2 * g, brows, :], rd.k[2 * g + 1, brows, :]], axis=0)
        q4 = jnp.concatenate([rd.q[rows, (PAIRS_PER_KV * g + i) * LANES:(PAIRS_PER_KV * g + i + 1) * LANES]
                              for i in range(PAIRS_PER_KV)], axis=0)
        return lax.dot_general(q4, kt, (((1,), (1,)), ((), ())), preferred_element_type=F32)

    def softmax(nb, g, s_ref):
        mask = first_mask if nb == 0 else band_mask
        probs = []
        for i in range(PAIRS_PER_KV):
            halves = []
            for hf in range(2):
                sink = c.sinks[2 * (PAIRS_PER_KV * g + i) + hf]
                sh = s_ref[i * ATT_BLOCK:(i + 1) * ATT_BLOCK, hf * BAND:(hf + 1) * BAND]
                sh = jnp.where(mask, sh, jnp.where(sink_col, sink, -jnp.inf))
                m = jnp.max(sh, axis=-1, keepdims=True)
                p = jnp.exp(sh - m)
                halves.append((p / jnp.sum(p, axis=-1, keepdims=True)).astype(BF16))
            probs.append(jnp.concatenate(halves, axis=1))
        return jnp.concatenate(probs, axis=0)

    def weighted_values(nb, g, probs):
        rows = slice(nb * ATT_BLOCK, (nb + 1) * ATT_BLOCK)
        brows = slice(nb * ATT_BLOCK, nb * ATT_BLOCK + BAND)
        vt = jnp.concatenate([rd.v[2 * g, brows, :], rd.v[2 * g + 1, brows, :]], axis=0)
        vt = jnp.where(v_keep, vt, jnp.zeros_like(vt))
        o4 = _dot(probs, vt)
        for i in range(PAIRS_PER_KV):
            p_idx = PAIRS_PER_KV * g + i
            rd.ab[rows, p_idx * LANES:(p_idx + 1) * LANES] = o4[i * ATT_BLOCK:(i + 1) * ATT_BLOCK].astype(BF16)

    n_blocks = ts // ATT_BLOCK
    stages = [(nb, g) for nb in range(n_blocks) for g in range(N_KV_HEADS)]

    for i, (nb, g) in enumerate(stages):
        raw.s[i] = scores(nb, g)
    raw.a[...] = _dot(h, c.w_in[:, 0:2 * D_MODEL])
    raw.qkv[...] = _dot(h, c.w_in[:, 2 * D_MODEL:2 * D_MODEL + Q_W + 2 * KV_W])
    raw.gl[...] = _dot(h, c.w_in[:, 2 * D_MODEL + Q_W + 2 * KV_W:])
    for i, (nb, g) in enumerate(stages):
        weighted_values(nb, g, softmax(nb, g, raw.s.at[i]))
    raw.ya[...] = _dot(rd.ab[...], c.w_ao[...])

    def u_token(i):
        return rd.u[i * SUBLANES:(i + 1) * SUBLANES, :]

    first_tap = CONV_HALO - (CONV_WIDTH - 1)
    for t0 in range(0, ts, CONV_TOKENS):
        accs = [c.dwb[0]] * CONV_TOKENS
        for k in range(CONV_WIDTH):
            w_k = c.dww[k]
            accs = [acc + w_k * u_token(t0 + j + first_tap + k) for j, acc in enumerate(accs)]
        for j, acc in enumerate(accs):
            rd.y[(t0 + j) * SUBLANES:(t0 + j + 1) * SUBLANES, :] = acc
    y = jnp.concatenate(
        [jnp.concatenate([rd.y[pl.ds(tg * SUBLANES * SUBLANES + lc, SUBLANES, stride=SUBLANES), :]
                          for lc in range(D_MODEL // LANES)], axis=1)
         for tg in range(ts // SUBLANES)], axis=0)
    mu = jnp.mean(y, axis=-1, keepdims=True)
    yc = y - mu
    var = jnp.mean(yc * yc, axis=-1, keepdims=True)
    yn = yc * lax.rsqrt(var + LN_EPS) * c.lng[...] + c.lnb[...]
    raw.yc[...] = _dot((yn * jax.nn.sigmoid(yn)).astype(BF16), c.w_co[...])

    tail = rd.u[ts * SUBLANES:(ts + CONV_HALO) * SUBLANES, :]
    wr.u[0:CONV_HALO * SUBLANES, :] = jnp.where(first_p, jnp.zeros_like(tail), tail)
    u = raw.a[:, :D_MODEL] * jax.nn.sigmoid(raw.a[:, D_MODEL:])
    for tg in range(ts // SUBLANES):
        for lc in range(D_MODEL // LANES):
            wr.u[pl.ds((CONV_HALO + tg * SUBLANES) * SUBLANES + lc, SUBLANES, stride=SUBLANES), :] = (
                u[tg * SUBLANES:(tg + 1) * SUBLANES, lc * LANES:(lc + 1) * LANES])

    ang = pos_ref[0].astype(F32) * c.invf[...]
    lane = lax.broadcasted_iota(jnp.int32, (ts, LANES), 1)
    first_half = (lane & (HEAD_DIM // 2)) == 0
    cos = jnp.cos(ang)
    sin = jnp.sin(ang)
    sin_signed = jnp.where(first_half, -sin, sin)

    def rope(t):
        swapped = jnp.where(first_half, pltpu.roll(t, LANES - HEAD_DIM // 2, axis=1),
                            pltpu.roll(t, HEAD_DIM // 2, axis=1))
        return t * cos + swapped * sin_signed

    for p in range(N_PAIRS):
        cols = slice(p * LANES, (p + 1) * LANES)
        wr.q[:, cols] = (rope(raw.qkv[:, cols]) * (HEAD_DIM ** -0.5)).astype(BF16)
    k_rot = rope(raw.qkv[:, Q_W:Q_W + KV_W])
    v_raw = raw.qkv[:, Q_W + KV_W:Q_W + 2 * KV_W]
    low = lane < HEAD_DIM
    zero = jnp.zeros((ts, LANES), F32)
    cur = slice(ATT_BLOCK, ATT_BLOCK + ts)
    for dst, src, t in ((wr.k, rd.k, k_rot), (wr.v, rd.v, v_raw)):
        t_sw = pltpu.roll(t, HEAD_DIM, axis=1)
        dst[:, 0:ATT_BLOCK, :] = src[:, ts:ts + ATT_BLOCK, :]
        dst[0, cur, :] = jnp.where(low, t, zero).astype(BF16)
        dst[1, cur, :] = jnp.where(low, zero, t_sw).astype(BF16)
        dst[2, cur, :] = jnp.where(low, t_sw, zero).astype(BF16)
        dst[3, cur, :] = jnp.where(low, zero, t).astype(BF16)
    wr.g[...] = jax.nn.sigmoid(raw.gl[...] + c.b_gate[...])

    merged = rd.g[:, :D_MODEL] * raw.yc[...] + rd.g[:, D_MODEL:] * raw.ya[...]
    raw.o[...] = _dot(merged.astype(BF16), c.w_o[...])
    out_ref[...] = rd.x[...] + _rms_norm(raw.o[...], c.g_post[...])


MixConsts = collections.namedtuple(
    "MixConsts", "sinks invf g_pre w_in b_gate dww dwb lng lnb w_co w_ao w_o g_post")


def _mix_kernel(tiles_per_seq, sinks_ref, x_odd_ref, pos_odd_ref, x_even_ref, pos_even_ref, invf_ref, g_pre_ref,
                w_in_ref, b_gate_ref, dww_ref, dwb_ref, lng_ref, lnb_ref, w_co_ref, w_ao_ref, w_o_ref,
                g_post_ref, o_ref, *scratch):
    consts = MixConsts(sinks_ref, invf_ref, g_pre_ref, w_in_ref, b_gate_ref, dww_ref, dwb_ref, lng_ref, lnb_ref,
                       w_co_ref, w_ao_ref, w_o_ref, g_post_ref)
    n_fields = len(MixState._fields)
    even, odd = MixState(*scratch[:n_fields]), MixState(*scratch[n_fields:2 * n_fields])
    raw = MixRaw(*scratch[2 * n_fields:])
    j = pl.program_id(0)

    @pl.when(j == 0)
    def _():
        for ref in (even.u, even.q, even.k, even.v, even.g, even.x):
            ref[...] = jnp.zeros(ref.shape, ref.dtype)

    finished_starts_seq = lax.rem(2 * j - 2, tiles_per_seq) == 0
    prepared_starts_seq = lax.rem(2 * j, tiles_per_seq) == 0
    _mix_pair(x_odd_ref, pos_odd_ref, o_ref.at[0, 0], odd, even, raw, False, finished_starts_seq, consts)
    _mix_pair(x_even_ref, pos_even_ref, o_ref.at[0, 1], even, odd, raw, prepared_starts_seq, False, consts)


def _ffn_kernel(x_ref, g_pre_ref, w_up_ref, dww_ref, dwb_ref, w_dn_ref, g_post_ref, o_ref, pbuf):
    ts = FFN_TILE
    sid = pl.program_id(1)

    @pl.when(sid == 0)
    def _():
        pbuf[:, 0:FFN_HALO, :] = jnp.zeros((D_FF // FFN_CHUNK, FFN_HALO, 2 * FFN_CHUNK), F32)

    x = x_ref[0]
    h = _rms_norm(x, g_pre_ref[...]).astype(BF16)
    n = ts + FFN_HALO
    n_chunks = D_FF // FFN_CHUNK
    chunk_cols = lambda c: slice(c * 2 * FFN_CHUNK, (c + 1) * 2 * FFN_CHUNK)
    acc = jnp.zeros((ts, D_MODEL), F32)
    for c in range(min(FFN_LOOKAHEAD, n_chunks)):
        pbuf[c, FFN_HALO:n, :] = _dot(h, w_up_ref[:, chunk_cols(c)])
    for c in range(n_chunks):
        cols = chunk_cols(c)
        if c + FFN_LOOKAHEAD < n_chunks:
            pbuf[c + FFN_LOOKAHEAD, FFN_HALO:n, :] = _dot(h, w_up_ref[:, chunk_cols(c + FFN_LOOKAHEAD)])
        p = pbuf[c]
        up = dwb_ref[:, cols] + dww_ref[2:3, cols] * p[FFN_HALO:]
        up = up + dww_ref[1:2, cols] * pltpu.roll(p, n - (FFN_HALO - 1), axis=0)[:ts]
        up = up + dww_ref[0:1, cols] * pltpu.roll(p, n - (FFN_HALO - 2), axis=0)[:ts]
        pbuf[c, 0:FFN_HALO, :] = pbuf[c, ts:n, :]
        gate = up[:, :FFN_CHUNK]
        z = gate * jax.nn.sigmoid(gate) * up[:, FFN_CHUNK:]
        acc = acc + _dot(z.astype(BF16), w_dn_ref[c * FFN_CHUNK:(c + 1) * FFN_CHUNK, :])
    o_ref[0] = x + _rms_norm(acc, g_post_ref[...])


def _interleave_halves(w, chunk):
    half = w.shape[-1] // 2
    lead = w.shape[:-1]
    g = w[..., :half].reshape(*lead, half // chunk, chunk)
    v = w[..., half:].reshape(*lead, half // chunk, chunk)
    return jnp.concatenate([g, v], axis=-1).reshape(*lead, 2 * half)


def _mix_call(x, positions, inv_freq, ln_pre, w_in, b_gate, dw_w, dw_b, ln_g, ln_b, w_co, sinks, w_ao, w_o, ln_post):
    bsz, seq, _ = x.shape
    ts = MIX_TILE
    tiles_per_seq = seq // ts
    n_tiles = bsz * tiles_per_seq
    assert seq % ts == 0 and tiles_per_seq % 2 == 0 and ts % ATT_BLOCK == 0
    x_tiles = x.reshape(n_tiles, ts, D_MODEL)
    pos_tiles = positions.reshape(n_tiles, ts, 1)

    const = lambda shape: pl.BlockSpec(shape, lambda j: (0,) * len(shape), pipeline_mode=pl.Buffered(1))
    odd_tile = lambda w: pl.BlockSpec((1, ts, w), lambda j: (jnp.maximum(2 * j - 1, 0), 0, 0))
    even_tile = lambda w: pl.BlockSpec((1, ts, w), lambda j: (jnp.minimum(2 * j, n_tiles - 1), 0, 0))
    row = lambda v: v.reshape(1, -1)

    state = [
        pltpu.VMEM(((CONV_HALO + ts) * SUBLANES, LANES), F32),
        pltpu.VMEM((ts, Q_W), BF16),
        pltpu.VMEM((4, ATT_BLOCK + ts, LANES), BF16),
        pltpu.VMEM((4, ATT_BLOCK + ts, LANES), BF16),
        pltpu.VMEM((ts, 2 * D_MODEL), F32),
        pltpu.VMEM((ts, D_MODEL), F32),
        pltpu.VMEM((ts * SUBLANES, LANES), F32),
        pltpu.VMEM((ts, Q_W), BF16),
    ]
    n_stages = (ts // ATT_BLOCK) * N_KV_HEADS
    raw = [
        pltpu.VMEM((n_stages, PAIRS_PER_KV * ATT_BLOCK, 2 * BAND), F32),
        pltpu.VMEM((ts, 2 * D_MODEL), F32),
        pltpu.VMEM((ts, Q_W + 2 * KV_W), F32),
        pltpu.VMEM((ts, 2 * D_MODEL), F32),
        pltpu.VMEM((ts, D_MODEL), F32),
        pltpu.VMEM((ts, D_MODEL), F32),
        pltpu.VMEM((ts, D_MODEL), F32),
    ]
    out = pl.pallas_call(
        functools.partial(_mix_kernel, tiles_per_seq),
        grid=(n_tiles // 2 + 1,),
        in_specs=[
            pl.BlockSpec(memory_space=pltpu.SMEM),
            odd_tile(D_MODEL), odd_tile(1), even_tile(D_MODEL), even_tile(1),
            const((1, LANES)),
            const((1, D_MODEL)),
            const(w_in.shape),
            const((1, 2 * D_MODEL)),
            const((CONV_WIDTH, SUBLANES, LANES)),
            const((1, SUBLANES, LANES)),
            const((1, D_MODEL)),
            const((1, D_MODEL)),
            const((D_MODEL, D_MODEL)),
            const((Q_W, D_MODEL)),
            const((D_MODEL, D_MODEL)),
            const((1, D_MODEL)),
        ],
        out_specs=pl.BlockSpec((1, 2, ts, D_MODEL), lambda j: (jnp.maximum(j - 1, 0), 0, 0, 0)),
        out_shape=jax.ShapeDtypeStruct((n_tiles // 2, 2, ts, D_MODEL), F32),
        scratch_shapes=state + state + raw,
        compiler_params=pltpu.CompilerParams(dimension_semantics=("arbitrary",),
                                             vmem_limit_bytes=VMEM_LIMIT_BYTES),
        name="mix",
    )(sinks, x_tiles, pos_tiles, x_tiles, pos_tiles, inv_freq, row(ln_pre), w_in.astype(BF16), row(b_gate),
      dw_w.reshape(CONV_WIDTH, SUBLANES, LANES), dw_b.reshape(1, SUBLANES, LANES), row(ln_g), row(ln_b), w_co.astype(BF16), w_ao.astype(BF16), w_o.astype(BF16), row(ln_post))
    return out.reshape(bsz, seq, D_MODEL)


def _ffn_call(x, ln_pre, w_up, dw_w, dw_b, w_down, ln_post):
    bsz, seq, _ = x.shape
    ts = FFN_TILE
    assert seq % ts == 0
    const = lambda shape: pl.BlockSpec(shape, lambda b, s: (0,) * len(shape), pipeline_mode=pl.Buffered(1))
    tile = pl.BlockSpec((1, ts, D_MODEL), lambda b, s: (b, s, 0))
    row = lambda v: v.reshape(1, -1)
    return pl.pallas_call(
        _ffn_kernel,
        grid=(bsz, seq // ts),
        in_specs=[
            tile,
            const((1, D_MODEL)),
            const((D_MODEL, 2 * D_FF)),
            const((FFN_CONV_WIDTH, 2 * D_FF)),
            const((1, 2 * D_FF)),
            const((D_FF, D_MODEL)),
            const((1, D_MODEL)),
        ],
        out_specs=tile,
        out_shape=jax.ShapeDtypeStruct(x.shape, F32),
        scratch_shapes=[pltpu.VMEM((D_FF // FFN_CHUNK, FFN_HALO + ts, 2 * FFN_CHUNK), F32)],
        compiler_params=pltpu.CompilerParams(dimension_semantics=("arbitrary", "arbitrary"),
                                             vmem_limit_bytes=VMEM_LIMIT_BYTES),
        name="ffn",
    )(x, row(ln_pre), _interleave_halves(w_up, FFN_CHUNK).astype(BF16), _interleave_halves(dw_w, FFN_CHUNK),
      row(_interleave_halves(dw_b, FFN_CHUNK)), w_down.astype(BF16), row(ln_post))


def kernel(x, positions, ln_mix_pre, w_in, b_gate, conv_dw_w, conv_dw_b, conv_ln_g, conv_ln_b, w_conv_out, attn_sinks, w_attn_out, w_out, ln_mix_post, ln_ffn_pre, w_up, ffn_dw_w, ffn_dw_b, w_down, ln_ffn_post):
    assert x.shape[-1] == D_MODEL
    inv_freq = ROPE_THETA ** (-jnp.arange(0, HEAD_DIM, 2, dtype=F32) / HEAD_DIM)
    inv_freq = jnp.tile(inv_freq, LANES // (HEAD_DIM // 2))[None, :]
    for l in range(w_in.shape[0]):
        x = _mix_call(x, positions, inv_freq, ln_mix_pre[l], w_in[l], b_gate[l], conv_dw_w[l], conv_dw_b[l],
                      conv_ln_g[l], conv_ln_b[l], w_conv_out[l], attn_sinks[l], w_attn_out[l], w_out[l],
                      ln_mix_post[l])
        x = _ffn_call(x, ln_ffn_pre[l], w_up[l], ffn_dw_w[l], ffn_dw_b[l], w_down[l], ln_ffn_post[l])
    return x
```

```python
import collections
import functools

import jax
import jax.numpy as jnp
from jax import lax
from jax.experimental import pallas as pl
from jax.experimental.pallas import tpu as pltpu

F32 = jnp.float32
BF16 = jnp.bfloat16

LANES = 128
SUBLANES = 8
VMEM_LIMIT_BYTES = 60000 * 1024

D_MODEL = 1024
HEAD_DIM = 64
N_Q_HEADS = 16
N_KV_HEADS = 2
ATT_BLOCK = 128
ROPE_THETA = 10000.0
CONV_WIDTH = 31
D_FF = 2816
FFN_CONV_WIDTH = 3
RMS_EPS = 1e-6
LN_EPS = 1e-5
Q_W = N_Q_HEADS * HEAD_DIM
KV_W = N_KV_HEADS * HEAD_DIM

MIX_TILE = 256
CONV_HALO = 32
CONV_TOKENS = 4
FFN_TILE = 256
FFN_TILES_PER_STEP = 1
FFN_HALO = SUBLANES
FFN_CHUNK = 256
FFN_LOOKAHEAD = 4
N_PAIRS = N_Q_HEADS // 2
PAIRS_PER_KV = N_PAIRS // N_KV_HEADS
BAND = 2 * ATT_BLOCK

MixState = collections.namedtuple("MixState", "u q k v g x y ab")
MixRaw = collections.namedtuple("MixRaw", "s a qkv gl ya yc o")


def _rms_norm(x, g):
    return x * lax.rsqrt(jnp.mean(x * x, axis=-1, keepdims=True) + RMS_EPS) * g


def _dot(a, b):
    return jnp.dot(a, b, preferred_element_type=F32)


def _mix_pair(x_ref, pos_ref, out_ref, wr, rd, raw, first_p, first_f, c):
    ts = MIX_TILE
    x = x_ref[0]
    wr.x[...] = x
    h = _rms_norm(x, c.g_pre[...]).astype(BF16)

    qi = lax.broadcasted_iota(jnp.int32, (ATT_BLOCK, BAND), 0)
    kj = lax.broadcasted_iota(jnp.int32, (ATT_BLOCK, BAND), 1)
    rel = qi + ATT_BLOCK - kj
    band_mask = (rel >= 0) & (rel < ATT_BLOCK)
    first_mask = band_mask & (kj >= jnp.where(first_f, ATT_BLOCK, 0))
    sink_col = kj == 0
    vrow = lax.broadcasted_iota(jnp.int32, (2 * BAND, LANES), 0)
    v_keep = (vrow != 0) & (vrow != BAND)

    def scores(nb, g):
        rows = slice(nb * ATT_BLOCK, (nb + 1) * ATT_BLOCK)
        brows = slice(nb * ATT_BLOCK, nb * ATT_BLOCK + BAND)
        kt = jnp.concatenate([rd.k[2 * g, brows, :], rd.k[2 * g + 1, brows, :]], axis=0)
        q4 = jnp.concatenate([rd.q[rows, (PAIRS_PER_KV * g + i) * LANES:(PAIRS_PER_KV * g + i + 1) * LANES]
                              for i in range(PAIRS_PER_KV)], axis=0)
        return lax.dot_general(q4, kt, (((1,), (1,)), ((), ())), preferred_element_type=F32)

    def softmax(nb, g, s_ref):
        mask = first_mask if nb == 0 else band_mask
        probs = []
        for i in range(PAIRS_PER_KV):
            halves = []
            for hf in range(2):
                sink = c.sinks[2 * (PAIRS_PER_KV * g + i) + hf]
                sh = s_ref[i * ATT_BLOCK:(i + 1) * ATT_BLOCK, hf * BAND:(hf + 1) * BAND]
                sh = jnp.where(mask, sh, jnp.where(sink_col, sink, -jnp.inf))
                m = jnp.max(sh, axis=-1, keepdims=True)
                p = jnp.exp(sh - m)
                halves.append((p / jnp.sum(p, axis=-1, keepdims=True)).astype(BF16))
            probs.append(jnp.concatenate(halves, axis=1))
        return jnp.concatenate(probs, axis=0)

    def weighted_values(nb, g, probs):
        rows = slice(nb * ATT_BLOCK, (nb + 1) * ATT_BLOCK)
        brows = slice(nb * ATT_BLOCK, nb * ATT_BLOCK + BAND)
        vt = jnp.concatenate([rd.v[2 * g, brows, :], rd.v[2 * g + 1, brows, :]], axis=0)
        vt = jnp.where(v_keep, vt, jnp.zeros_like(vt))
        o4 = _dot(probs, vt)
        for i in range(PAIRS_PER_KV):
            p_idx = PAIRS_PER_KV * g + i
            rd.ab[rows, p_idx * LANES:(p_idx + 1) * LANES] = o4[i * ATT_BLOCK:(i + 1) * ATT_BLOCK].astype(BF16)

    n_blocks = ts // ATT_BLOCK
    stages = [(nb, g) for nb in range(n_blocks) for g in range(N_KV_HEADS)]

    for i, (nb, g) in enumerate(stages):
        raw.s[i] = scores(nb, g)
    raw.a[:, :D_MODEL] = _dot(h, c.w_in[:, 0:D_MODEL])
    raw.a[:, D_MODEL:] = _dot(h, c.w_in[:, D_MODEL:2 * D_MODEL])
    raw.qkv[...] = _dot(h, c.w_in[:, 2 * D_MODEL:2 * D_MODEL + Q_W + 2 * KV_W])
    gate_col = 2 * D_MODEL + Q_W + 2 * KV_W
    raw.gl[:, :D_MODEL] = _dot(h, c.w_in[:, gate_col:gate_col + D_MODEL])
    raw.gl[:, D_MODEL:] = _dot(h, c.w_in[:, gate_col + D_MODEL:])
    for i, (nb, g) in enumerate(stages):
        weighted_values(nb, g, softmax(nb, g, raw.s.at[i]))
    raw.ya[...] = _dot(rd.ab[...], c.w_ao[...])

    def u_token(i):
        return rd.u[i * SUBLANES:(i + 1) * SUBLANES, :]

    first_tap = CONV_HALO - (CONV_WIDTH - 1)
    for t0 in range(0, ts, CONV_TOKENS):
        accs = [c.dwb[0]] * CONV_TOKENS
        for k in range(CONV_WIDTH):
            w_k = c.dww[k]
            accs = [acc + w_k * u_token(t0 + j + first_tap + k) for j, acc in enumerate(accs)]
        for j, acc in enumerate(accs):
            rd.y[(t0 + j) * SUBLANES:(t0 + j + 1) * SUBLANES, :] = acc
    y = jnp.concatenate(
        [jnp.concatenate([rd.y[pl.ds(tg * SUBLANES * SUBLANES + lc, SUBLANES, stride=SUBLANES), :]
                          for lc in range(D_MODEL // LANES)], axis=1)
         for tg in range(ts // SUBLANES)], axis=0)
    mu = jnp.mean(y, axis=-1, keepdims=True)
    yc = y - mu
    var = jnp.mean(yc * yc, axis=-1, keepdims=True)
    yn = yc * lax.rsqrt(var + LN_EPS) * c.lng[...] + c.lnb[...]
    raw.yc[...] = _dot((yn * jax.nn.sigmoid(yn)).astype(BF16), c.w_co[...])

    tail = rd.u[ts * SUBLANES:(ts + CONV_HALO) * SUBLANES, :]
    wr.u[0:CONV_HALO * SUBLANES, :] = jnp.where(first_p, jnp.zeros_like(tail), tail)
    u = raw.a[:, :D_MODEL] * jax.nn.sigmoid(raw.a[:, D_MODEL:])
    for tg in range(ts // SUBLANES):
        for lc in range(D_MODEL // LANES):
            wr.u[pl.ds((CONV_HALO + tg * SUBLANES) * SUBLANES + lc, SUBLANES, stride=SUBLANES), :] = (
                u[tg * SUBLANES:(tg + 1) * SUBLANES, lc * LANES:(lc + 1) * LANES])

    ang = pos_ref[0].astype(F32) * c.invf[...]
    lane = lax.broadcasted_iota(jnp.int32, (ts, LANES), 1)
    first_half = (lane & (HEAD_DIM // 2)) == 0
    cos = jnp.cos(ang)
    sin = jnp.sin(ang)
    sin_signed = jnp.where(first_half, -sin, sin)

    def rope(t):
        swapped = jnp.where(first_half, pltpu.roll(t, LANES - HEAD_DIM // 2, axis=1),
                            pltpu.roll(t, HEAD_DIM // 2, axis=1))
        return t * cos + swapped * sin_signed

    for p in range(N_PAIRS):
        cols = slice(p * LANES, (p + 1) * LANES)
        wr.q[:, cols] = (rope(raw.qkv[:, cols]) * (HEAD_DIM ** -0.5)).astype(BF16)
    k_rot = rope(raw.qkv[:, Q_W:Q_W + KV_W])
    v_raw = raw.qkv[:, Q_W + KV_W:Q_W + 2 * KV_W]
    low = lane < HEAD_DIM
    zero = jnp.zeros((ts, LANES), F32)
    cur = slice(ATT_BLOCK, ATT_BLOCK + ts)
    for dst, src, t in ((wr.k, rd.k, k_rot), (wr.v, rd.v, v_raw)):
        t_sw = pltpu.roll(t, HEAD_DIM, axis=1)
        dst[:, 0:ATT_BLOCK, :] = src[:, ts:ts + ATT_BLOCK, :]
        dst[0, cur, :] = jnp.where(low, t, zero).astype(BF16)
        dst[1, cur, :] = jnp.where(low, zero, t_sw).astype(BF16)
        dst[2, cur, :] = jnp.where(low, t_sw, zero).astype(BF16)
        dst[3, cur, :] = jnp.where(low, zero, t).astype(BF16)
    wr.g[...] = jax.nn.sigmoid(raw.gl[...] + c.b_gate[...])

    merged = rd.g[:, :D_MODEL] * raw.yc[...] + rd.g[:, D_MODEL:] * raw.ya[...]
    raw.o[...] = _dot(merged.astype(BF16), c.w_o[...])
    out_ref[...] = rd.x[...] + _rms_norm(raw.o[...], c.g_post[...])


MixConsts = collections.namedtuple(
    "MixConsts", "sinks invf g_pre w_in b_gate dww dwb lng lnb w_co w_ao w_o g_post")


def _mix_kernel(tiles_per_seq, sinks_ref, x_odd_ref, pos_odd_ref, x_even_ref, pos_even_ref, invf_ref, g_pre_ref,
                w_in_ref, b_gate_ref, dww_ref, dwb_ref, lng_ref, lnb_ref, w_co_ref, w_ao_ref, w_o_ref,
                g_post_ref, o_ref, *scratch):
    consts = MixConsts(sinks_ref, invf_ref, g_pre_ref, w_in_ref, b_gate_ref, dww_ref, dwb_ref, lng_ref, lnb_ref,
                       w_co_ref, w_ao_ref, w_o_ref, g_post_ref)
    n_fields = len(MixState._fields)
    even, odd = MixState(*scratch[:n_fields]), MixState(*scratch[n_fields:2 * n_fields])
    raw = MixRaw(*scratch[2 * n_fields:])
    j = pl.program_id(0)

    @pl.when(j == 0)
    def _():
        for ref in (even.u, even.q, even.k, even.v, even.g, even.x):
            ref[...] = jnp.zeros(ref.shape, ref.dtype)

    finished_starts_seq = lax.rem(2 * j - 2, tiles_per_seq) == 0
    prepared_starts_seq = lax.rem(2 * j, tiles_per_seq) == 0
    _mix_pair(x_odd_ref, pos_odd_ref, o_ref.at[0, 0], odd, even, raw, False, finished_starts_seq, consts)
    _mix_pair(x_even_ref, pos_even_ref, o_ref.at[0, 1], even, odd, raw, prepared_starts_seq, False, consts)


def _ffn_kernel(x_ref, g_pre_ref, w_up_ref, dww_ref, dwb_ref, w_dn_ref, g_post_ref, o_ref, pbuf, acc_ref):
    ts = FFN_TILE
    sid = pl.program_id(1)

    @pl.when(sid == 0)
    def _():
        pbuf[:, 0:FFN_HALO, :] = jnp.zeros((D_FF // FFN_CHUNK, FFN_HALO, 2 * FFN_CHUNK), F32)

    n = ts + FFN_HALO
    n_chunks = D_FF // FFN_CHUNK
    chunk_cols = lambda c: slice(c * 2 * FFN_CHUNK, (c + 1) * 2 * FFN_CHUNK)
    tile_rows = lambda t: slice(t * ts, (t + 1) * ts)
    normed = {}

    def h_of(t):
        if t not in normed:
            normed[t] = _rms_norm(x_ref[0, tile_rows(t), :], g_pre_ref[...]).astype(BF16)
        return normed[t]

    def up_proj(q):
        t, c = divmod(q, n_chunks)
        pbuf[c, FFN_HALO:n, :] = _dot(h_of(t), w_up_ref[:, chunk_cols(c)])

    total = FFN_TILES_PER_STEP * n_chunks
    for q in range(min(FFN_LOOKAHEAD, total)):
        up_proj(q)
    for q in range(total):
        t, c = divmod(q, n_chunks)
        cols = chunk_cols(c)
        if q + FFN_LOOKAHEAD < total:
            up_proj(q + FFN_LOOKAHEAD)
        p = pbuf[c]
        up = dwb_ref[:, cols] + dww_ref[2:3, cols] * p[FFN_HALO:]
        up = up + dww_ref[1:2, cols] * pltpu.roll(p, n - (FFN_HALO - 1), axis=0)[:ts]
        up = up + dww_ref[0:1, cols] * pltpu.roll(p, n - (FFN_HALO - 2), axis=0)[:ts]
        pbuf[c, 0:FFN_HALO, :] = pbuf[c, ts:n, :]
        gate = up[:, :FFN_CHUNK]
        z = gate * jax.nn.sigmoid(gate) * up[:, FFN_CHUNK:]
        down = _dot(z.astype(BF16), w_dn_ref[c * FFN_CHUNK:(c + 1) * FFN_CHUNK, :])
        if c == 0:
            acc_ref[...] = down
        else:
            acc_ref[...] += down
        if c == n_chunks - 1:
            o_ref[0, tile_rows(t), :] = x_ref[0, tile_rows(t), :] + _rms_norm(acc_ref[...], g_post_ref[...])


def _interleave_halves(w, chunk):
    half = w.shape[-1] // 2
    lead = w.shape[:-1]
    g = w[..., :half].reshape(*lead, half // chunk, chunk)
    v = w[..., half:].reshape(*lead, half // chunk, chunk)
    return jnp.concatenate([g, v], axis=-1).reshape(*lead, 2 * half)


def _mix_call(x, positions, inv_freq, ln_pre, w_in, b_gate, dw_w, dw_b, ln_g, ln_b, w_co, sinks, w_ao, w_o, ln_post):
    bsz, seq, _ = x.shape
    ts = MIX_TILE
    tiles_per_seq = seq // ts
    n_tiles = bsz * tiles_per_seq
    assert seq % ts == 0 and tiles_per_seq % 2 == 0 and ts % ATT_BLOCK == 0
    x_tiles = x.reshape(n_tiles, ts, D_MODEL)
    pos_tiles = positions.reshape(n_tiles, ts, 1)

    const = lambda shape: pl.BlockSpec(shape, lambda j: (0,) * len(shape), pipeline_mode=pl.Buffered(1))
    odd_tile = lambda w: pl.BlockSpec((1, ts, w), lambda j: (jnp.maximum(2 * j - 1, 0), 0, 0))
    even_tile = lambda w: pl.BlockSpec((1, ts, w), lambda j: (jnp.minimum(2 * j, n_tiles - 1), 0, 0))
    row = lambda v: v.reshape(1, -1)

    state = [
        pltpu.VMEM(((CONV_HALO + ts) * SUBLANES, LANES), F32),
        pltpu.VMEM((ts, Q_W), BF16),
        pltpu.VMEM((4, ATT_BLOCK + ts, LANES), BF16),
        pltpu.VMEM((4, ATT_BLOCK + ts, LANES), BF16),
        pltpu.VMEM((ts, 2 * D_MODEL), F32),
        pltpu.VMEM((ts, D_MODEL), F32),
        pltpu.VMEM((ts * SUBLANES, LANES), F32),
        pltpu.VMEM((ts, Q_W), BF16),
    ]
    n_stages = (ts // ATT_BLOCK) * N_KV_HEADS
    raw = [
        pltpu.VMEM((n_stages, PAIRS_PER_KV * ATT_BLOCK, 2 * BAND), F32),
        pltpu.VMEM((ts, 2 * D_MODEL), F32),
        pltpu.VMEM((ts, Q_W + 2 * KV_W), F32),
        pltpu.VMEM((ts, 2 * D_MODEL), F32),
        pltpu.VMEM((ts, D_MODEL), F32),
        pltpu.VMEM((ts, D_MODEL), F32),
        pltpu.VMEM((ts, D_MODEL), F32),
    ]
    out = pl.pallas_call(
        functools.partial(_mix_kernel, tiles_per_seq),
        grid=(n_tiles // 2 + 1,),
        in_specs=[
            pl.BlockSpec(memory_space=pltpu.SMEM),
            odd_tile(D_MODEL), odd_tile(1), even_tile(D_MODEL), even_tile(1),
            const((1, LANES)),
            const((1, D_MODEL)),
            const(w_in.shape),
            const((1, 2 * D_MODEL)),
            const((CONV_WIDTH, SUBLANES, LANES)),
            const((1, SUBLANES, LANES)),
            const((1, D_MODEL)),
            const((1, D_MODEL)),
            const((D_MODEL, D_MODEL)),
            const((Q_W, D_MODEL)),
            const((D_MODEL, D_MODEL)),
            const((1, D_MODEL)),
        ],
        out_specs=pl.BlockSpec((1, 2, ts, D_MODEL), lambda j: (jnp.maximum(j - 1, 0), 0, 0, 0)),
        out_shape=jax.ShapeDtypeStruct((n_tiles // 2, 2, ts, D_MODEL), F32),
        scratch_shapes=state + state + raw,
        compiler_params=pltpu.CompilerParams(dimension_semantics=("arbitrary",),
                                             vmem_limit_bytes=VMEM_LIMIT_BYTES),
        name="mix",
    )(sinks, x_tiles, pos_tiles, x_tiles, pos_tiles, inv_freq, row(ln_pre), w_in.astype(BF16), row(b_gate),
      dw_w.reshape(CONV_WIDTH, SUBLANES, LANES), dw_b.reshape(1, SUBLANES, LANES), row(ln_g), row(ln_b), w_co.astype(BF16), w_ao.astype(BF16), w_o.astype(BF16), row(ln_post))
    return out.reshape(bsz, seq, D_MODEL)


def _ffn_call(x, ln_pre, w_up, dw_w, dw_b, w_down, ln_post):
    bsz, seq, _ = x.shape
    ts = FFN_TILE * FFN_TILES_PER_STEP
    assert seq % ts == 0
    const = lambda shape: pl.BlockSpec(shape, lambda b, s: (0,) * len(shape), pipeline_mode=pl.Buffered(1))
    tile = pl.BlockSpec((1, ts, D_MODEL), lambda b, s: (b, s, 0))
    row = lambda v: v.reshape(1, -1)
    return pl.pallas_call(
        _ffn_kernel,
        grid=(bsz, seq // ts),
        in_specs=[
            tile,
            const((1, D_MODEL)),
            const((D_MODEL, 2 * D_FF)),
            const((FFN_CONV_WIDTH, 2 * D_FF)),
            const((1, 2 * D_FF)),
            const((D_FF, D_MODEL)),
            const((1, D_MODEL)),
        ],
        out_specs=tile,
        out_shape=jax.ShapeDtypeStruct(x.shape, F32),
        scratch_shapes=[pltpu.VMEM((D_FF // FFN_CHUNK, FFN_HALO + FFN_TILE, 2 * FFN_CHUNK), F32),
                        pltpu.VMEM((FFN_TILE, D_MODEL), F32)],
        compiler_params=pltpu.CompilerParams(dimension_semantics=("arbitrary", "arbitrary"),
                                             vmem_limit_bytes=VMEM_LIMIT_BYTES),
        name="ffn",
    )(x, row(ln_pre), _interleave_halves(w_up, FFN_CHUNK).astype(BF16), _interleave_halves(dw_w, FFN_CHUNK),
      row(_interleave_halves(dw_b, FFN_CHUNK)), w_down.astype(BF16), row(ln_post))


def kernel(x, positions, ln_mix_pre, w_in, b_gate, conv_dw_w, conv_dw_b, conv_ln_g, conv_ln_b, w_conv_out, attn_sinks, w_attn_out, w_out, ln_mix_post, ln_ffn_pre, w_up, ffn_dw_w, ffn_dw_b, w_down, ln_ffn_post):
    assert x.shape[-1] == D_MODEL
    inv_freq = ROPE_THETA ** (-jnp.arange(0, HEAD_DIM, 2, dtype=F32) / HEAD_DIM)
    inv_freq = jnp.tile(inv_freq, LANES // (HEAD_DIM // 2))[None, :]
    for l in range(w_in.shape[0]):
        x = _mix_call(x, positions, inv_freq, ln_mix_pre[l], w_in[l], b_gate[l], conv_dw_w[l], conv_dw_b[l],
                      conv_ln_g[l], conv_ln_b[l], w_conv_out[l], attn_sinks[l], w_attn_out[l], w_out[l],
                      ln_mix_post[l])
        x = _ffn_call(x, ln_ffn_pre[l], w_up[l], ffn_dw_w[l], ffn_dw_b[l], w_down[l], ln_ffn_post[l])
    return x
```

```python
import collections
import functools

import jax
import jax.numpy as jnp
from jax import lax
from jax.experimental import pallas as pl
from jax.experimental.pallas import tpu as pltpu

F32 = jnp.float32
BF16 = jnp.bfloat16

LANES = 128
SUBLANES = 8
VMEM_LIMIT_BYTES = 60000 * 1024

D_MODEL = 1024
HEAD_DIM = 64
N_Q_HEADS = 16
N_KV_HEADS = 2
ATT_BLOCK = 128
ROPE_THETA = 10000.0
CONV_WIDTH = 31
D_FF = 2816
FFN_CONV_WIDTH = 3
RMS_EPS = 1e-6
LN_EPS = 1e-5
Q_W = N_Q_HEADS * HEAD_DIM
KV_W = N_KV_HEADS * HEAD_DIM

MIX_TILE = 256
CONV_HALO = 32
CONV_TOKENS = 4
FFN_TILE = 256
FFN_TILES_PER_STEP = 1
FFN_HALO = SUBLANES
FFN_CHUNK = 256
FFN_LOOKAHEAD = 4
N_PAIRS = N_Q_HEADS // 2
PAIRS_PER_KV = N_PAIRS // N_KV_HEADS
BAND = 2 * ATT_BLOCK

MixState = collections.namedtuple("MixState", "u q k v g x y ab")
MixRaw = collections.namedtuple("MixRaw", "s a qkv gl ya yc o")


def _rms_norm(x, g):
    return x * lax.rsqrt(jnp.mean(x * x, axis=-1, keepdims=True) + RMS_EPS) * g


def _dot(a, b):
    return jnp.dot(a, b, preferred_element_type=F32)


def _mix_pair(x_ref, pos_ref, out_ref, wr, rd, raw, first_p, first_f, c):
    ts = MIX_TILE
    x = x_ref[0]
    wr.x[...] = x
    h = _rms_norm(x, c.g_pre[...]).astype(BF16)

    qi = lax.broadcasted_iota(jnp.int32, (ATT_BLOCK, BAND), 0)
    kj = lax.broadcasted_iota(jnp.int32, (ATT_BLOCK, BAND), 1)
    rel = qi + ATT_BLOCK - kj
    band_mask = (rel >= 0) & (rel < ATT_BLOCK)
    first_mask = band_mask & (kj >= jnp.where(first_f, ATT_BLOCK, 0))
    sink_col = kj == 0
    vrow = lax.broadcasted_iota(jnp.int32, (2 * BAND, LANES), 0)
    v_keep = (vrow != 0) & (vrow != BAND)

    def scores(nb, g):
        rows = slice(nb * ATT_BLOCK, (nb + 1) * ATT_BLOCK)
        brows = slice(nb * ATT_BLOCK, nb * ATT_BLOCK + BAND)
        kt = jnp.concatenate([rd.k[2 * g, brows, :], rd.k[2 * g + 1, brows, :]], axis=0)
        q4 = jnp.concatenate([rd.q[rows, (PAIRS_PER_KV * g + i) * LANES:(PAIRS_PER_KV * g + i + 1) * LANES]
                              for i in range(PAIRS_PER_KV)], axis=0)
        return lax.dot_general(q4, kt, (((1,), (1,)), ((), ())), preferred_element_type=F32)

    def softmax(nb, g, s_ref):
        mask = first_mask if nb == 0 else band_mask
        probs = []
        for i in range(PAIRS_PER_KV):
            halves = []
            for hf in range(2):
                sink = c.sinks[2 * (PAIRS_PER_KV * g + i) + hf]
                sh = s_ref[i * ATT_BLOCK:(i + 1) * ATT_BLOCK, hf * BAND:(hf + 1) * BAND]
                sh = jnp.where(mask, sh, jnp.where(sink_col, sink, -jnp.inf))
                m = jnp.max(sh, axis=-1, keepdims=True)
                p = jnp.exp(sh - m)
                halves.append((p / jnp.sum(p, axis=-1, keepdims=True)).astype(BF16))
            probs.append(jnp.concatenate(halves, axis=1))
        return jnp.concatenate(probs, axis=0)

    def weighted_values(nb, g, probs):
        rows = slice(nb * ATT_BLOCK, (nb + 1) * ATT_BLOCK)
        brows = slice(nb * ATT_BLOCK, nb * ATT_BLOCK + BAND)
        vt = jnp.concatenate([rd.v[2 * g, brows, :], rd.v[2 * g + 1, brows, :]], axis=0)
        vt = jnp.where(v_keep, vt, jnp.zeros_like(vt))
        o4 = _dot(probs, vt)
        for i in range(PAIRS_PER_KV):
            p_idx = PAIRS_PER_KV * g + i
            rd.ab[rows, p_idx * LANES:(p_idx + 1) * LANES] = o4[i * ATT_BLOCK:(i + 1) * ATT_BLOCK].astype(BF16)

    n_blocks = ts // ATT_BLOCK
    stages = [(nb, g) for nb in range(n_blocks) for g in range(N_KV_HEADS)]

    for i, (nb, g) in enumerate(stages):
        raw.s[i] = scores(nb, g)
    raw.a[:, :D_MODEL] = _dot(h, c.w_in[:, 0:D_MODEL])
    raw.a[:, D_MODEL:] = _dot(h, c.w_in[:, D_MODEL:2 * D_MODEL])
    raw.qkv[...] = _dot(h, c.w_in[:, 2 * D_MODEL:2 * D_MODEL + Q_W + 2 * KV_W])
    gate_col = 2 * D_MODEL + Q_W + 2 * KV_W
    raw.gl[:, :D_MODEL] = _dot(h, c.w_in[:, gate_col:gate_col + D_MODEL])
    raw.gl[:, D_MODEL:] = _dot(h, c.w_in[:, gate_col + D_MODEL:])
    for i, (nb, g) in enumerate(stages):
        weighted_values(nb, g, softmax(nb, g, raw.s.at[i]))
    raw.ya[...] = _dot(rd.ab[...], c.w_ao[...])

    def u_token(i):
        return rd.u[i * SUBLANES:(i + 1) * SUBLANES, :]

    first_tap = CONV_HALO - (CONV_WIDTH - 1)
    for t0 in range(0, ts, CONV_TOKENS):
        accs = [c.dwb[0]] * CONV_TOKENS
        for k in range(CONV_WIDTH):
            w_k = c.dww[k]
            accs = [acc + w_k * u_token(t0 + j + first_tap + k) for j, acc in enumerate(accs)]
        for j, acc in enumerate(accs):
            rd.y[(t0 + j) * SUBLANES:(t0 + j + 1) * SUBLANES, :] = acc
    y = jnp.concatenate(
        [jnp.concatenate([rd.y[pl.ds(tg * SUBLANES * SUBLANES + lc, SUBLANES, stride=SUBLANES), :]
                          for lc in range(D_MODEL // LANES)], axis=1)
         for tg in range(ts // SUBLANES)], axis=0)
    mu = jnp.mean(y, axis=-1, keepdims=True)
    yc = y - mu
    var = jnp.mean(yc * yc, axis=-1, keepdims=True)
    yn = yc * lax.rsqrt(var + LN_EPS) * c.lng[...] + c.lnb[...]
    raw.yc[...] = _dot((yn * jax.nn.sigmoid(yn)).astype(BF16), c.w_co[...])

    tail = rd.u[ts * SUBLANES:(ts + CONV_HALO) * SUBLANES, :]
    wr.u[0:CONV_HALO * SUBLANES, :] = jnp.where(first_p, jnp.zeros_like(tail), tail)
    u = raw.a[:, :D_MODEL] * jax.nn.sigmoid(raw.a[:, D_MODEL:])
    for tg in range(ts // SUBLANES):
        for lc in range(D_MODEL // LANES):
            wr.u[pl.ds((CONV_HALO + tg * SUBLANES) * SUBLANES + lc, SUBLANES, stride=SUBLANES), :] = (
                u[tg * SUBLANES:(tg + 1) * SUBLANES, lc * LANES:(lc + 1) * LANES])

    pos_rows = pos_ref[0].astype(F32)
    pos_cols = jnp.concatenate(
        [jnp.broadcast_to(pos_rows[r:r + 1, :], (LANES, LANES)).T for r in range(ts // LANES)], axis=0)
    ang = pos_cols * c.invf[...]
    lane = lax.broadcasted_iota(jnp.int32, (ts, LANES), 1)
    first_half = (lane & (HEAD_DIM // 2)) == 0
    cos = jnp.cos(ang)
    sin = jnp.sin(ang)
    sin_signed = jnp.where(first_half, -sin, sin)

    def rope(t):
        swapped = jnp.where(first_half, pltpu.roll(t, LANES - HEAD_DIM // 2, axis=1),
                            pltpu.roll(t, HEAD_DIM // 2, axis=1))
        return t * cos + swapped * sin_signed

    for p in range(N_PAIRS):
        cols = slice(p * LANES, (p + 1) * LANES)
        wr.q[:, cols] = (rope(raw.qkv[:, cols]) * (HEAD_DIM ** -0.5)).astype(BF16)
    k_rot = rope(raw.qkv[:, Q_W:Q_W + KV_W])
    v_raw = raw.qkv[:, Q_W + KV_W:Q_W + 2 * KV_W]
    low = lane < HEAD_DIM
    zero = jnp.zeros((ts, LANES), F32)
    cur = slice(ATT_BLOCK, ATT_BLOCK + ts)
    for dst, src, t in ((wr.k, rd.k, k_rot), (wr.v, rd.v, v_raw)):
        t_sw = pltpu.roll(t, HEAD_DIM, axis=1)
        dst[:, 0:ATT_BLOCK, :] = src[:, ts:ts + ATT_BLOCK, :]
        dst[0, cur, :] = jnp.where(low, t, zero).astype(BF16)
        dst[1, cur, :] = jnp.where(low, zero, t_sw).astype(BF16)
        dst[2, cur, :] = jnp.where(low, t_sw, zero).astype(BF16)
        dst[3, cur, :] = jnp.where(low, zero, t).astype(BF16)
    wr.g[...] = jax.nn.sigmoid(raw.gl[...] + c.b_gate[...])

    merged = rd.g[:, :D_MODEL] * raw.yc[...] + rd.g[:, D_MODEL:] * raw.ya[...]
    raw.o[...] = _dot(merged.astype(BF16), c.w_o[...])
    out_ref[...] = rd.x[...] + _rms_norm(raw.o[...], c.g_post[...])


MixConsts = collections.namedtuple(
    "MixConsts", "sinks invf g_pre w_in b_gate dww dwb lng lnb w_co w_ao w_o g_post")


def _mix_kernel(tiles_per_seq, sinks_ref, x_odd_ref, pos_odd_ref, x_even_ref, pos_even_ref, invf_ref, g_pre_ref,
                w_in_ref, b_gate_ref, dww_ref, dwb_ref, lng_ref, lnb_ref, w_co_ref, w_ao_ref, w_o_ref,
                g_post_ref, o_ref, *scratch):
    consts = MixConsts(sinks_ref, invf_ref, g_pre_ref, w_in_ref, b_gate_ref, dww_ref, dwb_ref, lng_ref, lnb_ref,
                       w_co_ref, w_ao_ref, w_o_ref, g_post_ref)
    n_fields = len(MixState._fields)
    even, odd = MixState(*scratch[:n_fields]), MixState(*scratch[n_fields:2 * n_fields])
    raw = MixRaw(*scratch[2 * n_fields:])
    j = pl.program_id(0)

    @pl.when(j == 0)
    def _():
        for ref in (even.u, even.q, even.k, even.v, even.g, even.x):
            ref[...] = jnp.zeros(ref.shape, ref.dtype)

    finished_starts_seq = lax.rem(2 * j - 2, tiles_per_seq) == 0
    prepared_starts_seq = lax.rem(2 * j, tiles_per_seq) == 0
    _mix_pair(x_odd_ref, pos_odd_ref, o_ref.at[0, 0], odd, even, raw, False, finished_starts_seq, consts)
    _mix_pair(x_even_ref, pos_even_ref, o_ref.at[0, 1], even, odd, raw, prepared_starts_seq, False, consts)


def _ffn_kernel(x_ref, g_pre_ref, w_up_ref, dww_ref, dwb_ref, w_dn_ref, g_post_ref, o_ref, pbuf, acc_ref):
    ts = FFN_TILE
    sid = pl.program_id(1)

    @pl.when(sid == 0)
    def _():
        pbuf[:, 0:FFN_HALO, :] = jnp.zeros((D_FF // FFN_CHUNK, FFN_HALO, 2 * FFN_CHUNK), F32)

    n = ts + FFN_HALO
    n_chunks = D_FF // FFN_CHUNK
    chunk_cols = lambda c: slice(c * 2 * FFN_CHUNK, (c + 1) * 2 * FFN_CHUNK)
    tile_rows = lambda t: slice(t * ts, (t + 1) * ts)
    normed = {}

    def h_of(t):
        if t not in normed:
            normed[t] = _rms_norm(x_ref[0, tile_rows(t), :], g_pre_ref[...]).astype(BF16)
        return normed[t]

    def up_proj(q):
        t, c = divmod(q, n_chunks)
        pbuf[c, FFN_HALO:n, :] = _dot(h_of(t), w_up_ref[:, chunk_cols(c)])

    total = FFN_TILES_PER_STEP * n_chunks
    for q in range(min(FFN_LOOKAHEAD, total)):
        up_proj(q)
    for q in range(total):
        t, c = divmod(q, n_chunks)
        cols = chunk_cols(c)
        if q + FFN_LOOKAHEAD < total:
            up_proj(q + FFN_LOOKAHEAD)
        p = pbuf[c]
        up = dwb_ref[:, cols] + dww_ref[2:3, cols] * p[FFN_HALO:]
        up = up + dww_ref[1:2, cols] * pltpu.roll(p, n - (FFN_HALO - 1), axis=0)[:ts]
        up = up + dww_ref[0:1, cols] * pltpu.roll(p, n - (FFN_HALO - 2), axis=0)[:ts]
        pbuf[c, 0:FFN_HALO, :] = pbuf[c, ts:n, :]
        gate = up[:, :FFN_CHUNK]
        z = gate * jax.nn.sigmoid(gate) * up[:, FFN_CHUNK:]
        down = _dot(z.astype(BF16), w_dn_ref[c * FFN_CHUNK:(c + 1) * FFN_CHUNK, :])
        if c == 0:
            acc_ref[...] = down
        else:
            acc_ref[...] += down
        if c == n_chunks - 1:
            o_ref[0, tile_rows(t), :] = x_ref[0, tile_rows(t), :] + _rms_norm(acc_ref[...], g_post_ref[...])


def _interleave_halves(w, chunk):
    half = w.shape[-1] // 2
    lead = w.shape[:-1]
    g = w[..., :half].reshape(*lead, half // chunk, chunk)
    v = w[..., half:].reshape(*lead, half // chunk, chunk)
    return jnp.concatenate([g, v], axis=-1).reshape(*lead, 2 * half)


def _mix_call(x, positions, inv_freq, ln_pre, w_in, b_gate, dw_w, dw_b, ln_g, ln_b, w_co, sinks, w_ao, w_o, ln_post):
    bsz, seq, _ = x.shape
    ts = MIX_TILE
    tiles_per_seq = seq // ts
    n_tiles = bsz * tiles_per_seq
    assert seq % ts == 0 and tiles_per_seq % 2 == 0 and ts % ATT_BLOCK == 0
    x_tiles = x.reshape(n_tiles, ts, D_MODEL)
    pos_tiles = positions.reshape(n_tiles, ts // LANES, LANES)

    const = lambda shape: pl.BlockSpec(shape, lambda j: (0,) * len(shape), pipeline_mode=pl.Buffered(1))
    odd_tile = lambda *blk: pl.BlockSpec((1, *blk), lambda j: (jnp.maximum(2 * j - 1, 0), 0, 0))
    even_tile = lambda *blk: pl.BlockSpec((1, *blk), lambda j: (jnp.minimum(2 * j, n_tiles - 1), 0, 0))
    pos_blk = (ts // LANES, LANES)
    row = lambda v: v.reshape(1, -1)

    state = [
        pltpu.VMEM(((CONV_HALO + ts) * SUBLANES, LANES), F32),
        pltpu.VMEM((ts, Q_W), BF16),
        pltpu.VMEM((4, ATT_BLOCK + ts, LANES), BF16),
        pltpu.VMEM((4, ATT_BLOCK + ts, LANES), BF16),
        pltpu.VMEM((ts, 2 * D_MODEL), F32),
        pltpu.VMEM((ts, D_MODEL), F32),
        pltpu.VMEM((ts * SUBLANES, LANES), F32),
        pltpu.VMEM((ts, Q_W), BF16),
    ]
    n_stages = (ts // ATT_BLOCK) * N_KV_HEADS
    raw = [
        pltpu.VMEM((n_stages, PAIRS_PER_KV * ATT_BLOCK, 2 * BAND), F32),
        pltpu.VMEM((ts, 2 * D_MODEL), F32),
        pltpu.VMEM((ts, Q_W + 2 * KV_W), F32),
        pltpu.VMEM((ts, 2 * D_MODEL), F32),
        pltpu.VMEM((ts, D_MODEL), F32),
        pltpu.VMEM((ts, D_MODEL), F32),
        pltpu.VMEM((ts, D_MODEL), F32),
    ]
    out = pl.pallas_call(
        functools.partial(_mix_kernel, tiles_per_seq),
        grid=(n_tiles // 2 + 1,),
        in_specs=[
            pl.BlockSpec(memory_space=pltpu.SMEM),
            odd_tile(ts, D_MODEL), odd_tile(*pos_blk), even_tile(ts, D_MODEL), even_tile(*pos_blk),
            const((1, LANES)),
            const((1, D_MODEL)),
            const(w_in.shape),
            const((1, 2 * D_MODEL)),
            const((CONV_WIDTH, SUBLANES, LANES)),
            const((1, SUBLANES, LANES)),
            const((1, D_MODEL)),
            const((1, D_MODEL)),
            const((D_MODEL, D_MODEL)),
            const((Q_W, D_MODEL)),
            const((D_MODEL, D_MODEL)),
            const((1, D_MODEL)),
        ],
        out_specs=pl.BlockSpec((1, 2, ts, D_MODEL), lambda j: (jnp.maximum(j - 1, 0), 0, 0, 0)),
        out_shape=jax.ShapeDtypeStruct((n_tiles // 2, 2, ts, D_MODEL), F32),
        scratch_shapes=state + state + raw,
        compiler_params=pltpu.CompilerParams(dimension_semantics=("arbitrary",),
                                             vmem_limit_bytes=VMEM_LIMIT_BYTES),
        name="mix",
    )(sinks, x_tiles, pos_tiles, x_tiles, pos_tiles, inv_freq, row(ln_pre), w_in.astype(BF16), row(b_gate),
      dw_w.reshape(CONV_WIDTH, SUBLANES, LANES), dw_b.reshape(1, SUBLANES, LANES), row(ln_g), row(ln_b), w_co.astype(BF16), w_ao.astype(BF16), w_o.astype(BF16), row(ln_post))
    return out.reshape(bsz, seq, D_MODEL)


def _ffn_call(x, ln_pre, w_up, dw_w, dw_b, w_down, ln_post):
    bsz, seq, _ = x.shape
    ts = FFN_TILE * FFN_TILES_PER_STEP
    assert seq % ts == 0
    const = lambda shape: pl.BlockSpec(shape, lambda b, s: (0,) * len(shape), pipeline_mode=pl.Buffered(1))
    tile = pl.BlockSpec((1, ts, D_MODEL), lambda b, s: (b, s, 0))
    row = lambda v: v.reshape(1, -1)
    return pl.pallas_call(
        _ffn_kernel,
        grid=(bsz, seq // ts),
        in_specs=[
            tile,
            const((1, D_MODEL)),
            const((D_MODEL, 2 * D_FF)),
            const((FFN_CONV_WIDTH, 2 * D_FF)),
            const((1, 2 * D_FF)),
            const((D_FF, D_MODEL)),
            const((1, D_MODEL)),
        ],
        out_specs=tile,
        out_shape=jax.ShapeDtypeStruct(x.shape, F32),
        scratch_shapes=[pltpu.VMEM((D_FF // FFN_CHUNK, FFN_HALO + FFN_TILE, 2 * FFN_CHUNK), F32),
                        pltpu.VMEM((FFN_TILE, D_MODEL), F32)],
        compiler_params=pltpu.CompilerParams(dimension_semantics=("arbitrary", "arbitrary"),
                                             vmem_limit_bytes=VMEM_LIMIT_BYTES),
        name="ffn",
    )(x, row(ln_pre), _interleave_halves(w_up, FFN_CHUNK).astype(BF16), _interleave_halves(dw_w, FFN_CHUNK),
      row(_interleave_halves(dw_b, FFN_CHUNK)), w_down.astype(BF16), row(ln_post))


def kernel(x, positions, ln_mix_pre, w_in, b_gate, conv_dw_w, conv_dw_b, conv_ln_g, conv_ln_b, w_conv_out, attn_sinks, w_attn_out, w_out, ln_mix_post, ln_ffn_pre, w_up, ffn_dw_w, ffn_dw_b, w_down, ln_ffn_post):
    assert x.shape[-1] == D_MODEL
    inv_freq = ROPE_THETA ** (-jnp.arange(0, HEAD_DIM, 2, dtype=F32) / HEAD_DIM)
    inv_freq = jnp.tile(inv_freq, LANES // (HEAD_DIM // 2))[None, :]
    for l in range(w_in.shape[0]):
        x = _mix_call(x, positions, inv_freq, ln_mix_pre[l], w_in[l], b_gate[l], conv_dw_w[l], conv_dw_b[l],
                      conv_ln_g[l], conv_ln_b[l], w_conv_out[l], attn_sinks[l], w_attn_out[l], w_out[l],
                      ln_mix_post[l])
        x = _ffn_call(x, ln_ffn_pre[l], w_up[l], ffn_dw_w[l], ffn_dw_b[l], w_down[l], ln_ffn_post[l])
    return x
```
